```python
import math
import jax
import jax.numpy as jnp
from jax import lax
import numpy as np

D_MODEL = 2048
BATCH = 2
SEQ = 4096
DEPTH = 4
DEC_BATCH = 128
DEC_SEQ = 1
PAST_LEN = 8192
PAGE_SIZE = 128

A_HEADS = 8
A_DH = 64
IDX_HEADS = 16
IDX_DH = 32
IDX_TOPK_MAX = 256
B_HEADS = 4
B_DH = 64
C_HEADS = 8
C_NOPE = 64
C_ROPE = 32
C_DV = 128
C_LAT = 128
N_MEM = 256
MEM_HEADS = 4
MEM_DH = 128
D_FF = 4 * D_MODEL
ROPE_THETA = 10000.0
EPS = 1e-6
Q_BLOCK = 128

IN_SPLITS = (A_HEADS * A_DH, A_DH, A_DH, IDX_HEADS * IDX_DH, IDX_DH, IDX_HEADS,
             B_HEADS * 2 * B_DH, 2 * B_DH, 2 * B_DH,
             C_HEADS * (C_NOPE + C_ROPE), C_LAT, C_ROPE)
N_IN = sum(IN_SPLITS)
MIX_WIDTH = A_HEADS * A_DH + B_HEADS * 2 * B_DH + C_HEADS * C_DV

kernel_name = 'hymba_style_dsa_diff_mla_decoder_step'


def rms_norm(x, g):
    xf = x.astype(jnp.float32)
    y = xf * lax.rsqrt(jnp.mean(xf * xf, axis=-1, keepdims=True) + EPS)
    return (y * g.astype(jnp.float32)).astype(x.dtype)


def rope(x, pos):
    half = x.shape[-1] // 2
    inv = ROPE_THETA ** (-jnp.arange(half, dtype=jnp.float32) / half)
    ang = pos.astype(jnp.float32)[:, None] * inv[None, :]
    cos = jnp.cos(ang)[:, None, :]
    sin = jnp.sin(ang)[:, None, :]
    xf = x.astype(jnp.float32)
    x1, x2 = xf[..., :half], xf[..., half:]
    return jnp.concatenate([x1 * cos - x2 * sin, x2 * cos + x1 * sin], axis=-1).astype(x.dtype)


def split_cols(z):
    outs, start = [], 0
    for n in IN_SPLITS:
        outs.append(z[..., start:start + n])
        start += n
    return outs


def take_rows(a, idx):
    return jax.vmap(lambda an, i: an[i])(a, idx)


def gather_pages(pool, page_table):
    g = pool[page_table]
    return g.reshape((page_table.shape[0], -1) + pool.shape[2:])


def gather_rows(pool, page_table, new, idx):
    N = idx.shape[0]
    past_len = page_table.shape[1] * pool.shape[1]
    ip = jnp.minimum(idx, past_len - 1)
    phys = jnp.take_along_axis(page_table, (ip // pool.shape[1]).reshape(N, -1), axis=1).reshape(idx.shape)
    from_past = pool[phys, ip % pool.shape[1]]
    from_new = take_rows(new, jnp.clip(idx - past_len, 0, new.shape[1] - 1))
    return jnp.where((idx < past_len)[..., None], from_past, from_new)


def map_query_blocks(fn, qpos, *xs):
    N, T = xs[0].shape[:2]
    nb = T // Q_BLOCK
    blk = tuple(jnp.swapaxes(a.reshape((N, nb, Q_BLOCK) + a.shape[2:]), 0, 1) for a in xs)
    out = lax.map(lambda b: fn(b[0], *b[1:]), (qpos.reshape(nb, Q_BLOCK),) + blk)
    return jnp.swapaxes(out, 0, 1).reshape((N, T) + out.shape[3:])


def prep_tokens(h, pos, lw):
    N, T, _ = h.shape
    qa, ka, va, qi, ki, wi, qb, kb, vb, qc, ckv, kr = split_cols(h @ lw['w_in'])
    qa = rope(rms_norm(qa.reshape(N, T, A_HEADS, A_DH), lw['g_qa']), pos)
    ka = rope(rms_norm(ka.reshape(N, T, 1, A_DH), lw['g_ka']), pos)[:, :, 0]
    qi = rope(qi.reshape(N, T, IDX_HEADS, IDX_DH), pos)
    ki = rope(rms_norm(ki.reshape(N, T, 1, IDX_DH), lw['g_ki']), pos)[:, :, 0]
    wi = wi * (IDX_HEADS * IDX_DH) ** -0.5
    qb = rope(rms_norm(qb.reshape(N, T, 2 * B_HEADS, B_DH), lw['g_qb']), pos).reshape(N, T, B_HEADS, 2, B_DH)
    kb = rope(rms_norm(kb.reshape(N, T, 2, B_DH), lw['g_kb']), pos)
    qc = rms_norm(qc.reshape(N, T, C_HEADS, C_NOPE + C_ROPE), lw['g_qc'])
    q_abs = jnp.einsum('nthd,chd->nthc', qc[..., :C_NOPE], lw['w_uk'])
    qr = rope(qc[..., C_NOPE:], pos)
    lat = rms_norm(ckv, lw['g_kv'])
    kr = rope(rms_norm(kr.reshape(N, T, 1, C_ROPE), lw['g_kr']), pos)[:, :, 0]
    return {'qa': qa, 'ka': ka, 'va': va, 'qi': qi, 'ki': ki, 'wi': wi,
            'qb': qb, 'kb': kb, 'vb': vb, 'q_abs': q_abs, 'qr': qr, 'lat': lat, 'kr': kr}


def dsa_attend(qpos, mask, q, qi, wi, ki_all, gather_kv, topk):
    N, Q = q.shape[:2]
    sc = jax.nn.relu(jnp.einsum('nqhd,nsd->nqhs', qi, ki_all).astype(jnp.float32))
    score = jnp.einsum('nqhs,nqh->nqs', sc, wi.astype(jnp.float32))
    score = jnp.where(mask[None], score, -jnp.inf)
    _, idx = lax.top_k(score, topk)
    valid = idx <= qpos[None, :, None]
    k_sel, v_sel = gather_kv(idx)
    s = jnp.einsum('nqhd,nqkd->nqhk', q, k_sel).astype(jnp.float32) * A_DH ** -0.5
    p = jax.nn.softmax(jnp.where(valid[:, :, None, :], s, -jnp.inf), axis=-1).astype(v_sel.dtype)
    return jnp.einsum('nqhk,nqkd->nqhd', p, v_sel).reshape(N, Q, A_HEADS * A_DH)


def diff_attend(mask, q, k, v, lam, lam_init, g_subln):
    N, Q = q.shape[:2]
    s = jnp.einsum('nqhcd,nscd->nchqs', q, k).astype(jnp.float32) * B_DH ** -0.5
    p = jax.nn.softmax(jnp.where(mask, s, -jnp.inf), axis=-1)
    a = (p[:, 0] - lam * p[:, 1]).astype(v.dtype)
    o = jnp.einsum('nhqs,nse->nqhe', a, v)
    o = rms_norm(o, g_subln) * (1.0 - lam_init)
    return o.reshape(N, Q, B_HEADS * 2 * B_DH)


def mla_attend(mask, q_abs, qr, lat, kr, w_uv):
    N, Q = q_abs.shape[:2]
    s = (jnp.einsum('nqhc,nsc->nhqs', q_abs, lat) + jnp.einsum('nqhr,nsr->nhqs', qr, kr)).astype(jnp.float32)
    s = s * (C_NOPE + C_ROPE) ** -0.5
    p = jax.nn.softmax(jnp.where(mask, s, -jnp.inf), axis=-1).astype(lat.dtype)
    o_lat = jnp.einsum('nhqs,nsc->nqhc', p, lat)
    return jnp.einsum('nqhc,chv->nqhv', o_lat, w_uv).reshape(N, Q, C_HEADS * C_DV)


def mixer_prompt(t, pos, lw, lam, lam_init):
    S = pos.shape[0]
    topk = min(IDX_TOPK_MAX, S // 4)
    key_pos = jnp.arange(S, dtype=jnp.int32)

    def gather_kv(idx):
        return take_rows(t['ka'], idx), take_rows(t['va'], idx)

    def block(qpos, qa, qi, wi, qb, q_abs, qr):
        mask = key_pos[None, :] <= qpos[:, None]
        oa = dsa_attend(qpos, mask, qa, qi, wi, t['ki'], gather_kv, topk)
        ob = diff_attend(mask, qb, t['kb'], t['vb'], lam, lam_init, lw['g_subln'])
        oc = mla_attend(mask, q_abs, qr, t['lat'], t['kr'], lw['w_uv'])
        return jnp.concatenate([oa, ob, oc], axis=-1)

    return map_query_blocks(block, pos, t['qa'], t['qi'], t['wi'], t['qb'], t['q_abs'], t['qr'])


def mixer_sample(t, pos, lw, caches, page_table, lam, lam_init):
    c_ak, c_av, c_ai, c_bk, c_bv, c_lat, c_kr = caches
    L = page_table.shape[1] * PAGE_SIZE + pos.shape[0]
    topk = min(IDX_TOPK_MAX, L // 4)
    key_pos = jnp.arange(L, dtype=jnp.int32)
    mask = key_pos[None, :] <= pos[:, None]
    ki_all = jnp.concatenate([gather_pages(c_ai, page_table), t['ki']], axis=1)
    kb_all = jnp.concatenate([gather_pages(c_bk, page_table), t['kb']], axis=1)
    vb_all = jnp.concatenate([gather_pages(c_bv, page_table), t['vb']], axis=1)
    lat_all = jnp.concatenate([gather_pages(c_lat, page_table), t['lat']], axis=1)
    kr_all = jnp.concatenate([gather_pages(c_kr, page_table), t['kr']], axis=1)

    def gather_kv(idx):
        return gather_rows(c_ak, page_table, t['ka'], idx), gather_rows(c_av, page_table, t['va'], idx)

    oa = dsa_attend(pos, mask, t['qa'], t['qi'], t['wi'], ki_all, gather_kv, topk)
    ob = diff_attend(mask, t['qb'], kb_all, vb_all, lam, lam_init, lw['g_subln'])
    oc = mla_attend(mask, t['q_abs'], t['qr'], lat_all, kr_all, lw['w_uv'])
    return jnp.concatenate([oa, ob, oc], axis=-1)


def mem_kv(mem, lw):
    N, M, _ = mem.shape
    hm = rms_norm(mem, lw['norm_memin'])
    mk = rms_norm((hm @ lw['w_mk']).reshape(N, M, MEM_HEADS, MEM_DH), lw['g_mk'])
    mv = (hm @ lw['w_mv']).reshape(N, M, MEM_HEADS, MEM_DH)
    return mk, mv


def mem_attend(h, mk, mv, lw):
    N, T, _ = h.shape
    q = rms_norm((h @ lw['w_mq']).reshape(N, T, MEM_HEADS, MEM_DH), lw['g_mq'])
    s = jnp.einsum('nthd,nmhd->nhtm', q, mk).astype(jnp.float32) * MEM_DH ** -0.5
    p = jax.nn.softmax(s, axis=-1).astype(mv.dtype)
    o = jnp.einsum('nhtm,nmhd->nthd', p, mv).reshape(N, T, MEM_HEADS * MEM_DH)
    return o @ lw['w_mo']


def tail(x, mk, mv, lw):
    x = x + mem_attend(rms_norm(x, lw['norm_cross']), mk, mv, lw)
    u = rms_norm(x, lw['norm_mlp']) @ lw['w_up']
    return x + jnp.square(jax.nn.relu(u)) @ lw['w_down']


def setup_inputs(seed: int = 0) -> dict:
    key = jax.random.key(seed)
    ks = iter(jax.random.split(key, 64))
    f32 = jnp.float32
    n_pages = PAST_LEN // PAGE_SIZE
    n_pool = (DEC_BATCH * n_pages * 5) // 4

    def raw(shape):
        return jax.random.normal(next(ks), shape, f32)

    def nrm(shape, scale):
        return jax.random.normal(next(ks), shape, f32) * scale

    def gain(n):
        return 1.0 + 0.02 * jax.random.normal(next(ks), (DEPTH, n), f32)

    x_prompt = raw((BATCH, SEQ, D_MODEL))
    x_sample = raw((DEC_BATCH, DEC_SEQ, D_MODEL))
    mem_prompt = raw((BATCH, N_MEM, D_MODEL))
    cache_a_k = raw((DEPTH, n_pool, PAGE_SIZE, A_DH))
    cache_a_v = raw((DEPTH, n_pool, PAGE_SIZE, A_DH))
    cache_a_idx = raw((DEPTH, n_pool, PAGE_SIZE, IDX_DH))
    cache_b_k = raw((DEPTH, n_pool, PAGE_SIZE, 2, B_DH))
    cache_b_v = raw((DEPTH, n_pool, PAGE_SIZE, 2 * B_DH))
    cache_c_lat = raw((DEPTH, n_pool, PAGE_SIZE, C_LAT))
    cache_c_rope = raw((DEPTH, n_pool, PAGE_SIZE, C_ROPE))
    cache_mem_k = raw((DEPTH, DEC_BATCH, N_MEM, MEM_HEADS, MEM_DH))
    cache_mem_v = raw((DEPTH, DEC_BATCH, N_MEM, MEM_HEADS, MEM_DH))
    perm = jax.random.permutation(next(ks), n_pool)
    page_table = perm[:DEC_BATCH * n_pages].reshape(DEC_BATCH, n_pages).astype(jnp.int32)
    return {
        'x_prompt': x_prompt, 'x_sample': x_sample, 'mem_prompt': mem_prompt,
        'cache_a_k': cache_a_k, 'cache_a_v': cache_a_v, 'cache_a_idx': cache_a_idx,
        'cache_b_k': cache_b_k, 'cache_b_v': cache_b_v,
        'cache_c_lat': cache_c_lat, 'cache_c_rope': cache_c_rope,
        'cache_mem_k': cache_mem_k, 'cache_mem_v': cache_mem_v, 'page_table': page_table,
        'norm_mix': gain(D_MODEL),
        'w_in': nrm((DEPTH, D_MODEL, N_IN), D_MODEL ** -0.5),
        'g_qa': gain(A_DH), 'g_ka': gain(A_DH), 'g_ki': gain(IDX_DH),
        'g_qb': gain(B_DH), 'g_kb': gain(B_DH),
        'lam_q1': nrm((DEPTH, B_DH), 0.1), 'lam_k1': nrm((DEPTH, B_DH), 0.1),
        'lam_q2': nrm((DEPTH, B_DH), 0.1), 'lam_k2': nrm((DEPTH, B_DH), 0.1),
        'g_subln': gain(2 * B_DH),
        'g_qc': gain(C_NOPE + C_ROPE), 'g_kv': gain(C_LAT), 'g_kr': gain(C_ROPE),
        'w_uk': nrm((DEPTH, C_LAT, C_HEADS, C_NOPE), C_LAT ** -0.5),
        'w_uv': nrm((DEPTH, C_LAT, C_HEADS, C_DV), C_LAT ** -0.5),
        'w_out': nrm((DEPTH, MIX_WIDTH, D_MODEL), MIX_WIDTH ** -0.5),
        'norm_cross': gain(D_MODEL), 'norm_memin': gain(D_MODEL),
        'w_mq': nrm((DEPTH, D_MODEL, MEM_HEADS * MEM_DH), D_MODEL ** -0.5),
        'w_mk': nrm((DEPTH, D_MODEL, MEM_HEADS * MEM_DH), D_MODEL ** -0.5),
        'w_mv': nrm((DEPTH, D_MODEL, MEM_HEADS * MEM_DH), D_MODEL ** -0.5),
        'g_mq': gain(MEM_DH), 'g_mk': gain(MEM_DH),
        'w_mo': nrm((DEPTH, MEM_HEADS * MEM_DH, D_MODEL), (MEM_HEADS * MEM_DH) ** -0.5),
        'norm_mlp': gain(D_MODEL),
        'w_up': nrm((DEPTH, D_MODEL, D_FF), D_MODEL ** -0.5),
        'w_down': nrm((DEPTH, D_FF, D_MODEL), D_FF ** -0.5),
    }


def reference(x_prompt, x_sample, mem_prompt, cache_a_k, cache_a_v, cache_a_idx, cache_b_k, cache_b_v,
              cache_c_lat, cache_c_rope, cache_mem_k, cache_mem_v, page_table,
              norm_mix, w_in, g_qa, g_ka, g_ki, g_qb, g_kb, lam_q1, lam_k1, lam_q2, lam_k2, g_subln,
              g_qc, g_kv, g_kr, w_uk, w_uv, w_out, norm_cross, norm_memin, w_mq, w_mk, w_mv, g_mq, g_mk,
              w_mo, norm_mlp, w_up, w_down):
    f32 = jnp.float32
    pos_p = jnp.arange(x_prompt.shape[1], dtype=jnp.int32)
    pos_s = page_table.shape[1] * PAGE_SIZE + jnp.arange(x_sample.shape[1], dtype=jnp.int32)
    xp, xs = x_prompt, x_sample
    rows_p, rows_s, mems_p = [], [], []
    for l in range(DEPTH):
        lw = {'w_in': w_in[l], 'g_qa': g_qa[l], 'g_ka': g_ka[l], 'g_ki': g_ki[l], 'g_qb': g_qb[l],
              'g_kb': g_kb[l], 'g_subln': g_subln[l], 'g_qc': g_qc[l], 'g_kv': g_kv[l], 'g_kr': g_kr[l],
              'w_uk': w_uk[l], 'w_uv': w_uv[l], 'norm_cross': norm_cross[l], 'norm_memin': norm_memin[l],
              'w_mq': w_mq[l], 'w_mk': w_mk[l], 'w_mv': w_mv[l], 'g_mq': g_mq[l], 'g_mk': g_mk[l],
              'w_mo': w_mo[l], 'norm_mlp': norm_mlp[l], 'w_up': w_up[l], 'w_down': w_down[l]}
        lam_init = 0.8 - 0.6 * math.exp(-0.3 * l)
        lam = (jnp.exp(jnp.sum(lam_q1[l].astype(f32) * lam_k1[l].astype(f32)))
               - jnp.exp(jnp.sum(lam_q2[l].astype(f32) * lam_k2[l].astype(f32))) + lam_init)
        tp = prep_tokens(rms_norm(xp, norm_mix[l]), pos_p, lw)
        xp = xp + mixer_prompt(tp, pos_p, lw, lam, lam_init) @ w_out[l]
        mk_p, mv_p = mem_kv(mem_prompt, lw)
        xp = tail(xp, mk_p, mv_p, lw)
        rows_p.append((tp['ka'], tp['va'], tp['ki'], tp['kb'], tp['vb'], tp['lat'], tp['kr']))
        mems_p.append((mk_p, mv_p))
        ts = prep_tokens(rms_norm(xs, norm_mix[l]), pos_s, lw)
        caches_l = (cache_a_k[l], cache_a_v[l], cache_a_idx[l], cache_b_k[l], cache_b_v[l],
                    cache_c_lat[l], cache_c_rope[l])
        xs = xs + mixer_sample(ts, pos_s, lw, caches_l, page_table, lam, lam_init) @ w_out[l]
        xs = tail(xs, cache_mem_k[l], cache_mem_v[l], lw)
        rows_s.append((ts['ka'], ts['va'], ts['ki'], ts['kb'], ts['vb'], ts['lat'], ts['kr']))

    def stk(lst, i):
        return jnp.stack([r[i] for r in lst], axis=0)

    return (xp, xs,
            stk(rows_p, 0), stk(rows_p, 1), stk(rows_p, 2), stk(rows_p, 3), stk(rows_p, 4), stk(rows_p, 5), stk(rows_p, 6),
            stk(mems_p, 0), stk(mems_p, 1),
            stk(rows_s, 0), stk(rows_s, 1), stk(rows_s, 2), stk(rows_s, 3), stk(rows_s, 4), stk(rows_s, 5), stk(rows_s, 6))
```

```python
import functools
import math

import jax
import jax.numpy as jnp
import numpy as np
from jax import lax
from jax.experimental import pallas as pl
from jax.experimental.pallas import tpu as pltpu

A_HEADS, A_DH = 8, 64
IDX_HEADS, IDX_DH, IDX_TOPK_MAX = 16, 32, 256
B_HEADS, B_DH = 4, 64
C_HEADS, C_NOPE, C_ROPE, C_DV, C_LAT = 8, 64, 32, 128, 128
MEM_HEADS, MEM_DH = 4, 128
PAGE_SIZE = 128
ROPE_THETA = 10000.0
EPS = 1e-6

LANES = 128
VMEM_CAP_BYTES = 60000 * 1024
LOG2E = 1.4426950408889634
NEG = -1e30
KEY_NINF = -2139095041
INT_MIN = -(2 ** 31)

F32 = jnp.float32
BF16 = jnp.bfloat16

Z_QA, Z_QB, Z_QI, Z_QCN, Z_QCR = 0, 512, 1024, 1536, 2048
Z_KA, Z_VA, Z_KB, Z_VB, Z_CKV, Z_KI, Z_KR, Z_WI = 2304, 2432, 2560, 2688, 2816, 2944, 3072, 3200
NZ = 3328


def _cparams(sem):
    return pltpu.CompilerParams(dimension_semantics=sem, vmem_limit_bytes=VMEM_CAP_BYTES)


def _dot_t(a, b):
    return lax.dot_general(a, b, (((1,), (1,)), ((), ())), preferred_element_type=F32)


def _dot(a, b):
    return jnp.dot(a, b, preferred_element_type=F32)


def _mm_body(*refs, has_norm, act, has_res):
    it = iter(refs)
    x_ref = next(it)
    g_ref = next(it) if has_norm else None
    w_ref = next(it)
    r_ref = next(it) if has_res else None
    o_ref = next(it)
    xn_ref = next(it) if has_norm else None
    if has_norm:
        @pl.when(pl.program_id(1) == 0)
        def _():
            x = x_ref[...].astype(F32)
            ms = jnp.mean(x * x, axis=-1, keepdims=True)
            xn_ref[...] = (x * lax.rsqrt(ms + EPS) * g_ref[...]).astype(BF16)
        xv = xn_ref[...]
    else:
        xv = x_ref[...]
    acc = _dot(xv, w_ref[...])
    if act == "relu2":
        acc = jnp.square(jnp.maximum(acc, 0.0))
    if has_res:
        acc = acc + r_ref[...]
    o_ref[...] = acc.astype(o_ref.dtype)


def _matmul(x, w, *, g=None, res=None, act=None, out_dtype=F32, tm, tn, name):
    M, K = x.shape
    N = w.shape[1]
    assert M % tm == 0 and N % tn == 0, (M, tm, N, tn)
    in_specs = [pl.BlockSpec((tm, K), lambda i, j: (i, 0))]
    args = [x]
    scratch = []
    if g is not None:
        in_specs.append(pl.BlockSpec((1, K), lambda i, j: (0, 0)))
        args.append(g.reshape(1, K).astype(F32))
        scratch.append(pltpu.VMEM((tm, K), BF16))
    in_specs.append(pl.BlockSpec((K, tn), lambda i, j: (0, j)))
    args.append(w)
    if res is not None:
        in_specs.append(pl.BlockSpec((tm, tn), lambda i, j: (i, j)))
        args.append(res)
    body = functools.partial(_mm_body, has_norm=g is not None, act=act, has_res=res is not None)
    return pl.pallas_call(
        body,
        grid=(M // tm, N // tn),
        in_specs=in_specs,
        out_specs=pl.BlockSpec((tm, tn), lambda i, j: (i, j)),
        out_shape=jax.ShapeDtypeStruct((M, N), out_dtype),
        scratch_shapes=scratch,
        compiler_params=_cparams(("arbitrary", "arbitrary")),
        name=name,
    )(*args)


def _lane_iota(width=LANES):
    return lax.broadcasted_iota(jnp.int32, (1, width), 1)


def _seg_sums(col, nseg):
    lane = _lane_iota()
    shift = int(math.log2(LANES // nseg))
    x2 = col * col
    return [jnp.sum(jnp.where((lane >> shift) == s, x2, 0.0), axis=-1, keepdims=True) for s in range(nseg)]


def _seg_select(vals):
    lane = _lane_iota()
    shift = int(math.log2(LANES // len(vals)))
    out = vals[-1]
    for s in range(len(vals) - 2, -1, -1):
        out = jnp.where((lane >> shift) == s, vals[s], out)
    return out


def _rope(y, cos, sin_signed, half):
    lane = _lane_iota()
    first = (lane & (2 * half - 1)) < half
    rot = jnp.where(first, pltpu.roll(y, LANES - half, 1), pltpu.roll(y, half, 1))
    return y * cos + rot * sin_signed


def _prep_body(z_ref, c64_ref, s64_ref, c32_ref, s32_ref,
               gqa_ref, gqb_ref, gqcn_ref, gqcr_ref, gka_ref, gkb_ref, gkv_ref, gki_ref, gkr_ref,
               wuk_ref,
               ka_o, va_o, ki_o, kb_o, vb_o, lat_o, kr_o,
               qa_o, qb_o, qc_o, qi_o, wi_o, kab_o, vab_o, kib_o, kbb_o, vbb_o, kcb_o):
    lane = _lane_iota()
    c64, s64, c32, s32 = c64_ref[...], s64_ref[...], c32_ref[...], s32_ref[...]

    def zcol(off):
        return z_ref[:, off:off + LANES]

    def inv_rms(ss, n):
        return lax.rsqrt(ss * (1.0 / n) + EPS)

    sa = A_DH ** -0.5 * LOG2E
    for j in range(A_HEADS // 2):
        col = zcol(Z_QA + LANES * j)
        inv = _seg_select([inv_rms(s, A_DH) for s in _seg_sums(col, 2)])
        o = _rope(col * inv * gqa_ref[...], c64, s64, A_DH // 2) * sa
        qa_o[2 * j] = o[:, :A_DH].astype(BF16)
        qa_o[2 * j + 1] = pltpu.roll(o, LANES - A_DH, 1)[:, :A_DH].astype(BF16)

    sb = B_DH ** -0.5 * LOG2E
    for h in range(B_HEADS):
        col = zcol(Z_QB + LANES * h)
        inv = _seg_select([inv_rms(s, B_DH) for s in _seg_sums(col, 2)])
        o = _rope(col * inv * gqb_ref[...], c64, s64, B_DH // 2) * sb
        qb_o[h] = jnp.where(lane < B_DH, o, 0.0).astype(BF16)
        qb_o[B_HEADS + h] = jnp.where(lane < B_DH, 0.0, o).astype(BF16)

    per = LANES // IDX_DH
    for j in range(IDX_HEADS // per):
        o = _rope(zcol(Z_QI + LANES * j), c32, s32, IDX_DH // 2)
        for r in range(per):
            piece = o if r == 0 else pltpu.roll(o, LANES - IDX_DH * r, 1)
            qi_o[per * j + r] = piece[:, :IDX_DH].astype(BF16)
    wi_o[...] = zcol(Z_WI) * (IDX_HEADS * IDX_DH) ** -0.5

    sc = (C_NOPE + C_ROPE) ** -0.5 * LOG2E
    ncols = [zcol(Z_QCN + LANES * j) for j in range(C_HEADS // 2)]
    rper = LANES // C_ROPE
    rcols = [zcol(Z_QCR + LANES * j) for j in range(C_HEADS // rper)]
    nsum = [s for c in ncols for s in _seg_sums(c, 2)]
    rsum = [s for c in rcols for s in _seg_sums(c, rper)]
    invs = [inv_rms(nsum[h] + rsum[h], C_NOPE + C_ROPE) for h in range(C_HEADS)]
    for j in range(C_HEADS // 2):
        nn = (ncols[j] * _seg_select(invs[2 * j:2 * j + 2]) * gqcn_ref[...]).astype(BF16)
        qabs = _dot(nn, wuk_ref[j]) * sc
        qc_o[2 * j, :, 0:C_LAT] = qabs[:, :C_LAT].astype(BF16)
        qc_o[2 * j + 1, :, 0:C_LAT] = qabs[:, C_LAT:].astype(BF16)
    for j in range(C_HEADS // rper):
        rn = rcols[j] * _seg_select(invs[rper * j:rper * (j + 1)]) * gqcr_ref[...]
        o = _rope(rn, c32, s32, C_ROPE // 2) * sc
        for r in range(rper):
            piece = o if r == 0 else pltpu.roll(o, LANES - C_ROPE * r, 1)
            qc_o[rper * j + r, :, C_LAT:2 * C_LAT] = jnp.where(lane < C_ROPE, piece, 0.0).astype(BF16)

    col = zcol(Z_KA)
    inv = inv_rms(jnp.sum(col * col, axis=-1, keepdims=True), A_DH)
    o = _rope(col * inv * gka_ref[...], c64, s64, A_DH // 2)[:, :A_DH]
    ka_o[...] = o
    kab_o[...] = o.astype(BF16)
    col = zcol(Z_VA)[:, :A_DH]
    va_o[...] = col
    vab_o[...] = col.astype(BF16)

    col = zcol(Z_KB)
    inv = _seg_select([inv_rms(s, B_DH) for s in _seg_sums(col, 2)])
    o = _rope(col * inv * gkb_ref[...], c64, s64, B_DH // 2)
    kb_o[...] = o
    kbb_o[...] = o.astype(BF16)
    col = zcol(Z_VB)
    vb_o[...] = col
    vbb_o[...] = col.astype(BF16)

    col = zcol(Z_CKV)
    lat = col * inv_rms(jnp.sum(col * col, axis=-1, keepdims=True), C_LAT) * gkv_ref[...]
    lat_o[...] = lat
    kcb_o[:, 0:C_LAT] = lat.astype(BF16)
    col = zcol(Z_KR)
    inv = inv_rms(jnp.sum(col * col, axis=-1, keepdims=True), C_ROPE)
    o = _rope(col * inv * gkr_ref[...], c32, s32, C_ROPE // 2)
    kr_o[...] = o[:, :C_ROPE]
    kcb_o[:, C_LAT:2 * C_LAT] = jnp.where(lane < C_ROPE, o, 0.0).astype(BF16)

    col = zcol(Z_KI)
    inv = inv_rms(jnp.sum(col * col, axis=-1, keepdims=True), IDX_DH)
    o = _rope(col * inv * gki_ref[...], c32, s32, IDX_DH // 2)[:, :IDX_DH]
    ki_o[...] = o
    kib_o[...] = o.astype(BF16)


def _prep(z, tabs, gains, wuk_pairs, *, tm):
    M = z.shape[0]
    row = lambda w: pl.BlockSpec((tm, w), lambda i: (i, 0))
    const = lambda shape: pl.BlockSpec(shape, lambda i: (0,) * len(shape))
    heads = lambda h, w: pl.BlockSpec((h, tm, w), lambda i: (0, i, 0))
    in_specs = [row(NZ)] + [row(LANES)] * 4 + [const((1, LANES))] * 9 + [const(wuk_pairs.shape)]
    out_defs = [
        (A_DH, F32, None), (A_DH, F32, None), (IDX_DH, F32, None), (2 * B_DH, F32, None),
        (2 * B_DH, F32, None), (C_LAT, F32, None), (C_ROPE, F32, None),
        (A_DH, BF16, A_HEADS), (2 * B_DH, BF16, 2 * B_HEADS), (2 * C_LAT, BF16, C_HEADS),
        (IDX_DH, BF16, IDX_HEADS), (LANES, F32, None),
        (A_DH, BF16, None), (A_DH, BF16, None), (IDX_DH, BF16, None), (2 * B_DH, BF16, None),
        (2 * B_DH, BF16, None), (2 * C_LAT, BF16, None),
    ]
    out_specs, out_shape = [], []
    for w, dt, h in out_defs:
        if h is None:
            out_specs.append(row(w))
            out_shape.append(jax.ShapeDtypeStruct((M, w), dt))
        else:
            out_specs.append(heads(h, w))
            out_shape.append(jax.ShapeDtypeStruct((h, M, w), dt))
    return pl.pallas_call(
        _prep_body,
        grid=(M // tm,),
        in_specs=in_specs,
        out_specs=out_specs,
        out_shape=out_shape,
        compiler_params=_cparams(("arbitrary",)),
        name="prep_tokens",
    )(z, *tabs, *gains, wuk_pairs)


def _flash_step(carry, s, v_chunk):
    m, l, acc = carry
    m_new = jnp.maximum(m, jnp.max(s, axis=-1, keepdims=True))
    alpha = jnp.exp2(m - m_new)
    p = jnp.exp2(s - m_new)
    l = alpha * l + jnp.sum(p, axis=-1, keepdims=True)
    acc = alpha * acc + _dot(p.astype(BF16), v_chunk)
    return m_new, l, acc


def _flash_init(rows, dv):
    return (jnp.full((rows, 1), NEG, F32), jnp.zeros((rows, 1), F32), jnp.zeros((rows, dv), F32))


def _causal_bias(i, j, tq, tk):
    qpos = i * tq + lax.broadcasted_iota(jnp.int32, (tq, 1), 0)
    kpos = j * tk + lax.broadcasted_iota(jnp.int32, (1, tk), 1)
    return jnp.where(kpos <= qpos, 0.0, NEG)


def _flash_causal(q, k_ref, v_of, i, *, heads, tq, tk, dv):
    rows = heads * tq

    def chunk(j, carry, bias):
        kc = k_ref[pl.ds(pl.multiple_of(j * tk, tk), tk), :]
        s = _dot_t(q, kc)
        if bias is not None:
            s = (s.reshape(heads, tq, tk) + bias[None]).reshape(rows, tk)
        return _flash_step(carry, s, v_of(j, kc))

    jd = (i * tq) // tk
    carry = lax.fori_loop(0, jd, lambda j, c: chunk(j, c, None), _flash_init(rows, dv))
    _, l, acc = chunk(jd, carry, _causal_bias(i, jd, tq, tk))
    return acc * (1.0 / l)


def _lam_value(lq1, lk1, lq2, lk2, lam_init):
    a = jnp.sum(lq1[...] * lk1[...], axis=-1, keepdims=True)
    b = jnp.sum(lq2[...] * lk2[...], axis=-1, keepdims=True)
    return jnp.exp(a) - jnp.exp(b) + lam_init


def _diff_combine(o, lam, gsub, lam_init):
    outs = []
    for h in range(B_HEADS):
        d = o[h] - lam * o[B_HEADS + h]
        ms = jnp.mean(d * d, axis=-1, keepdims=True)
        outs.append(d * lax.rsqrt(ms + EPS) * gsub * (1.0 - lam_init))
    return outs


def _attn_b_body(lq1, lk1, lq2, lk2, gsub_ref, q_ref, k_ref, v_ref, o_ref, *, tq, tk, lam_init):
    i = pl.program_id(1)
    heads = 2 * B_HEADS
    q = q_ref[...].reshape(heads * tq, 2 * B_DH)

    def v_of(j, kc):
        return v_ref[pl.ds(pl.multiple_of(j * tk, tk), tk), :]

    o = _flash_causal(q, k_ref, v_of, i, heads=heads, tq=tq, tk=tk, dv=2 * B_DH)
    o = o.reshape(heads, tq, 2 * B_DH)
    lam = _lam_value(lq1, lk1, lq2, lk2, lam_init)
    for h, d in enumerate(_diff_combine(o, lam, gsub_ref[...], lam_init)):
        o_ref[:, LANES * h:LANES * (h + 1)] = d.astype(o_ref.dtype)


def _attn_c_body(wuv_ref, q_ref, k_ref, o_ref, *, tq, tk):
    i = pl.program_id(1)
    q = q_ref[...].reshape(C_HEADS * tq, 2 * C_LAT)
    o = _flash_causal(q, k_ref, lambda j, kc: kc[:, :C_LAT], i, heads=C_HEADS, tq=tq, tk=tk, dv=C_LAT)
    o = o.reshape(C_HEADS, tq, C_LAT).astype(BF16)
    for h in range(C_HEADS):
        o_ref[:, C_DV * h:C_DV * (h + 1)] = _dot(o[h], wuv_ref[h]).astype(o_ref.dtype)


def _sortable_key(score):
    b = pltpu.bitcast(score, jnp.int32)
    key = b ^ ((b >> 31) & 0x7FFFFFFF)
    return jnp.where(key == -1, 0, key)


def _attn_a_body(qi_ref, wi_ref, ki_ref, q_ref, k_ref, v_ref, tri_ref, plo_ref, phi_ref, o_ref,
                 key_ref, wb_ref, *, tq, tk, topk):
    i = pl.program_id(1)
    jd = (i * tq) // tk
    nsub = tk // LANES

    w = wi_ref[...]
    for h in range(IDX_HEADS):
        wb_ref[h] = jnp.broadcast_to(w[:, h:h + 1], (tq, LANES))

    def idx_chunk(j, masked):
        kc = ki_ref[pl.ds(pl.multiple_of(j * tk, tk), tk), :]
        score = jnp.zeros((tq, tk), F32)
        for h in range(IDX_HEADS):
            sc = jnp.maximum(_dot_t(qi_ref[h], kc), 0.0)
            score = score + sc * jnp.concatenate([wb_ref[h]] * nsub, axis=-1)
        if masked:
            qpos = i * tq + lax.broadcasted_iota(jnp.int32, (tq, 1), 0)
            kpos = j * tk + lax.broadcasted_iota(jnp.int32, (1, tk), 1)
            score = jnp.where(kpos <= qpos, score, -jnp.inf)
        key_ref[j] = _sortable_key(score)

    def idx_loop(j, c):
        idx_chunk(j, False)
        return c

    lax.fori_loop(0, jd, idx_loop, 0)
    idx_chunk(jd, True)

    def count_ge(cand):
        candb = jnp.broadcast_to(cand, (tq, tk))

        def body(j, acc):
            ge = jnp.where(key_ref[j] >= candb, 1.0, 0.0)
            part = ge[:, 0:LANES]
            for c in range(1, nsub):
                part = part + ge[:, LANES * c:LANES * (c + 1)]
            return acc + part

        acc = lax.fori_loop(0, jd + 1, body, jnp.zeros((tq, LANES), F32))
        return jnp.sum(acc, axis=-1, keepdims=True)

    def bit_body(t, thr):
        cand = thr ^ jnp.left_shift(jnp.int32(1), 31 - t)
        return jnp.where(count_ge(cand) >= topk, cand, thr)

    thr = lax.fori_loop(0, 32, bit_body, jnp.full((tq, 1), INT_MIN, jnp.int32))
    need = topk - count_ge(thr + 1)
    thrb = jnp.broadcast_to(thr, (tq, tk))

    q = q_ref[...].reshape(A_HEADS * tq, A_DH)
    rows = A_HEADS * tq

    def att_chunk(j, carry):
        ties_seen, fc = carry
        kk = key_ref[j]
        tie = jnp.where(kk == thrb, 1.0, 0.0)
        before = _dot(tie.astype(BF16), tri_ref[...]) + ties_seen
        keep_tie = jnp.where(before < need, tie, 0.0)
        keep = jnp.where(kk > thrb, 1.0, keep_tie)
        keep = jnp.where(kk > KEY_NINF, keep, 0.0)
        bias = jnp.where(keep > 0.0, 0.0, NEG)
        kc = k_ref[pl.ds(pl.multiple_of(j * tk, tk), tk), :]
        vc = v_ref[pl.ds(pl.multiple_of(j * tk, tk), tk), :]
        s = (_dot_t(q, kc).reshape(A_HEADS, tq, tk) + bias[None]).reshape(rows, tk)
        return ties_seen + jnp.sum(tie, axis=-1, keepdims=True), _flash_step(fc, s, vc)

    _, (_, l, acc) = lax.fori_loop(0, jd + 1, att_chunk, (jnp.zeros((tq, 1), F32), _flash_init(rows, A_DH)))
    o = (acc * (1.0 / l)).reshape(A_HEADS, tq, A_DH).astype(BF16)
    for j in range(A_HEADS // 2):
        pair = _dot(o[2 * j], plo_ref[...]) + _dot(o[2 * j + 1], phi_ref[...])
        o_ref[:, LANES * j:LANES * (j + 1)] = pair.astype(o_ref.dtype)


def _place_mats():
    eye = np.eye(A_DH, dtype=np.float32)
    zero = np.zeros((A_DH, A_DH), np.float32)
    return (jnp.asarray(np.concatenate([eye, zero], 1), BF16), jnp.asarray(np.concatenate([zero, eye], 1), BF16))


def _strict_upper(n):
    return jnp.asarray(np.triu(np.ones((n, n), np.float32), 1), BF16)


def _prompt_tiles(S):
    tq = min(256, S)
    tk = min(512, S)
    assert S % tq == 0 and S % tk == 0 and tk % tq == 0
    return tq, tk


def _attn_prompt(t, lw, lam_init, *, nb, S):
    tq, tk = _prompt_tiles(S)
    nq = S // tq
    grid = (nb, nq)
    qspec = lambda h, w: pl.BlockSpec((h, tq, w), lambda n, i: (0, n * nq + i, 0))
    kspec = lambda w: pl.BlockSpec((S, w), lambda n, i: (n, 0))
    const = lambda shape: pl.BlockSpec(shape, lambda n, i: (0,) * len(shape))
    ospec = lambda w: pl.BlockSpec((tq, w), lambda n, i: (n * nq + i, 0))
    rows = nb * S
    cp = _cparams(("arbitrary", "arbitrary"))
    topk = min(IDX_TOPK_MAX, S // 4)
    assert tk >= topk

    plo, phi = _place_mats()
    oa = pl.pallas_call(
        functools.partial(_attn_a_body, tq=tq, tk=tk, topk=topk),
        grid=grid,
        in_specs=[qspec(IDX_HEADS, IDX_DH), pl.BlockSpec((tq, LANES), lambda n, i: (n * nq + i, 0)),
                  kspec(IDX_DH), qspec(A_HEADS, A_DH), kspec(A_DH), kspec(A_DH),
                  const((tk, tk)), const((A_DH, LANES)), const((A_DH, LANES))],
        out_specs=ospec(A_HEADS * A_DH),
        out_shape=jax.ShapeDtypeStruct((rows, A_HEADS * A_DH), BF16),
        scratch_shapes=[pltpu.VMEM((S // tk, tq, tk), jnp.int32), pltpu.VMEM((IDX_HEADS, tq, LANES), F32)],
        compiler_params=cp,
        name="attn_a_prompt",
    )(t["qi"], t["wi"], t["kib"], t["qa"], t["kab"], t["vab"], _strict_upper(tk), plo, phi)

    lam_rows = [lw[k].reshape(1, B_DH).astype(F32) for k in ("lam_q1", "lam_k1", "lam_q2", "lam_k2")]
    ob = pl.pallas_call(
        functools.partial(_attn_b_body, tq=tq, tk=tk, lam_init=lam_init),
        grid=grid,
        in_specs=[const((1, B_DH))] * 4 + [const((1, 2 * B_DH)), qspec(2 * B_HEADS, 2 * B_DH),
                                             kspec(2 * B_DH), kspec(2 * B_DH)],
        out_specs=ospec(B_HEADS * 2 * B_DH),
        out_shape=jax.ShapeDtypeStruct((rows, B_HEADS * 2 * B_DH), BF16),
        compiler_params=cp,
        name="attn_b_prompt",
    )(*lam_rows, lw["g_subln"].reshape(1, 2 * B_DH), t["qb"], t["kbb"], t["vbb"])

    oc = pl.pallas_call(
        functools.partial(_attn_c_body, tq=tq, tk=tk),
        grid=grid,
        in_specs=[const((C_HEADS, C_LAT, C_DV)), qspec(C_HEADS, 2 * C_LAT), kspec(2 * C_LAT)],
        out_specs=ospec(C_HEADS * C_DV),
        out_shape=jax.ShapeDtypeStruct((rows, C_HEADS * C_DV), BF16),
        compiler_params=cp,
        name="attn_c_prompt",
    )(lw["wuv_h"], t["qc"], t["kcb"])
    return jnp.concatenate([oa, ob, oc], axis=-1)


def _page_copies(pt_ref, seq, slot, layer, pools, bufs, sems, keys_on_lanes):
    def copies(p):
        pg = pt_ref[seq, p]
        win = pl.ds(pl.multiple_of(p * PAGE_SIZE, PAGE_SIZE), PAGE_SIZE)
        return [pltpu.make_async_copy(pool.at[layer, pg], buf.at[slot, :, win] if t else buf.at[slot, win],
                                      sem.at[slot])
                for pool, buf, sem, t in zip(pools, bufs, sems, keys_on_lanes)]
    return copies


def _paged_fetch(pt_ref, layer, pools, bufs, sems, keys_on_lanes, npages):
    n = pl.program_id(0)
    nseq = pl.num_programs(0)
    slot = lax.rem(n, 2)

    def start(seq, sl):
        cp = _page_copies(pt_ref, seq, sl, layer, pools, bufs, sems, keys_on_lanes)

        def body(p, c):
            for d in cp(p):
                d.start()
            return c
        lax.fori_loop(0, npages, body, 0)

    @pl.when(n == 0)
    def _():
        start(0, 0)

    @pl.when(n + 1 < nseq)
    def _():
        start(n + 1, 1 - slot)

    cp = _page_copies(pt_ref, n, slot, layer, pools, bufs, sems, keys_on_lanes)

    def wbody(p, c):
        for d in cp(p):
            d.wait()
        return c
    lax.fori_loop(0, npages, wbody, 0)
    return slot


def _row_dot(q, krow):
    return jnp.sum(q.astype(F32) * krow.astype(BF16).astype(F32), axis=-1, keepdims=True)


def _softmax_new(s, s_new, v_past, v_new, v_keys_on_lanes=False):
    m = jnp.maximum(jnp.max(s, axis=-1, keepdims=True), s_new)
    p = jnp.exp2(s - m)
    pn = jnp.exp2(s_new - m)
    l = jnp.sum(p, axis=-1, keepdims=True) + pn
    pv = _dot_t(p.astype(BF16), v_past) if v_keys_on_lanes else _dot(p.astype(BF16), v_past)
    o = pv + pn.astype(BF16).astype(F32) * v_new.astype(BF16).astype(F32)
    return o * (1.0 / l)


def _sattn_b_body(pt_ref, lq1, lk1, lq2, lk2, gsub_ref, q_ref, kn_ref, vn_ref, kpool, vpool, o_ref,
                  kbuf, vbuf, ksem, vsem, *, layer, npages, lam_init):
    slot = _paged_fetch(pt_ref, layer, (kpool, vpool), (kbuf, vbuf), (ksem, vsem), (True, False), npages)
    q = q_ref[0]
    s = _dot(q, kbuf[slot].astype(BF16))
    o = _softmax_new(s, _row_dot(q, kn_ref[0]), vbuf[slot].astype(BF16), vn_ref[0])
    lam = _lam_value(lq1, lk1, lq2, lk2, lam_init)
    d = o[0:B_HEADS] - lam * o[B_HEADS:2 * B_HEADS]
    ms = jnp.mean(d * d, axis=-1, keepdims=True)
    o_ref[0] = (d * lax.rsqrt(ms + EPS) * gsub_ref[...] * (1.0 - lam_init)).astype(o_ref.dtype)


def _sattn_c_body(pt_ref, q_ref, kn_ref, lpool, rpool, o_ref, lbuf, rbuf, lsem, rsem, *, layer, npages):
    slot = _paged_fetch(pt_ref, layer, (lpool, rpool), (lbuf, rbuf), (lsem, rsem), (False, True), npages)
    q = q_ref[0]
    lat = lbuf[slot].astype(BF16)
    s = _dot_t(q[:, :C_LAT], lat) + _dot(q[:, C_LAT:C_LAT + C_ROPE], rbuf[slot].astype(BF16))
    kn = kn_ref[0]
    s_new = jnp.sum(q.astype(F32) * kn.astype(F32), axis=-1, keepdims=True)
    o_ref[0] = _softmax_new(s, s_new, lat, kn[:, :C_LAT]).astype(o_ref.dtype)


def _sattn_a_body(pt_ref, qi_ref, wi_ref, kin_ref, q_ref, kn_ref, vn_ref, tri_ref, ipool, kpool, vpool, o_ref,
                  ibuf, kbuf, vbuf, isem, ksem, vsem, *, layer, npages, topk, tk):
    slot = _paged_fetch(pt_ref, layer, (ipool, kpool, vpool), (ibuf, kbuf, vbuf), (isem, ksem, vsem),
                        (True, True, True), npages)
    L = npages * PAGE_SIZE
    qi = qi_ref[0]
    w = wi_ref[0]
    sc = jnp.maximum(_dot(qi, ibuf[slot].astype(BF16)), 0.0)
    keys = _sortable_key(jnp.sum(sc * w, axis=0, keepdims=True))
    sc_new = jnp.maximum(_row_dot(qi, kin_ref[0]), 0.0)
    key_new = _sortable_key(jnp.sum(sc_new * w, axis=0, keepdims=True))

    def count_ge(cand):
        c = jnp.sum(jnp.where(keys >= cand, 1.0, 0.0), axis=-1, keepdims=True)
        return c + jnp.where(key_new >= cand, 1.0, 0.0)

    def bit_body(t, thr):
        cand = thr ^ jnp.left_shift(jnp.int32(1), 31 - t)
        return jnp.where(count_ge(cand) >= topk, cand, thr)

    thr = lax.fori_loop(0, 32, bit_body, jnp.full((1, 1), INT_MIN, jnp.int32))
    need = topk - count_ge(thr + 1)

    ties_seen = jnp.zeros((1, 1), F32)
    bias = []
    for c in range(L // tk):
        kk = keys[:, c * tk:(c + 1) * tk]
        tie = jnp.where(kk == thr, 1.0, 0.0)
        before = _dot(tie.astype(BF16), tri_ref[...]) + ties_seen
        keep = jnp.where(kk > thr, 1.0, jnp.where(before < need, tie, 0.0))
        bias.append(jnp.where(keep > 0.0, 0.0, NEG))
        ties_seen = ties_seen + jnp.sum(tie, axis=-1, keepdims=True)
    bias = jnp.concatenate(bias, axis=-1)
    keep_new = jnp.where(key_new > thr, 1.0, jnp.where(key_new == thr, jnp.where(ties_seen < need, 1.0, 0.0), 0.0))
    bias_new = jnp.where(keep_new > 0.0, 0.0, NEG)

    q = q_ref[0]
    s = _dot(q, kbuf[slot].astype(BF16)) + bias
    s_new = _row_dot(q, kn_ref[0]) + bias_new
    o_ref[0] = _softmax_new(s, s_new, vbuf[slot].astype(BF16), vn_ref[0], True).astype(o_ref.dtype)


def _attn_sample(ts, lw, caches, page_table, lam_init, layer):
    c_ak, c_av, c_ai, c_bk, c_bv, c_lat, c_kr = caches
    ns, npages = page_table.shape
    L = npages * PAGE_SIZE
    topk = min(IDX_TOPK_MAX, (L + 1) // 4)
    tk = min(512, L)
    assert L % tk == 0
    seq = lambda a, b: pl.BlockSpec((1, a, b), lambda n, pt: (n, 0, 0))
    const = lambda shape: pl.BlockSpec(shape, lambda n, pt: (0,) * len(shape))
    hbm = pl.BlockSpec(memory_space=pl.ANY)
    cp = _cparams(("arbitrary",))
    dsem = pltpu.SemaphoreType.DMA((2,))

    def call(body, in_specs, out_w, out_h, scratch, name, out_dtype=F32):
        return pl.pallas_call(
            body,
            grid_spec=pltpu.PrefetchScalarGridSpec(
                num_scalar_prefetch=1, grid=(ns,), in_specs=in_specs,
                out_specs=seq(out_h, out_w), scratch_shapes=scratch),
            out_shape=jax.ShapeDtypeStruct((ns, out_h, out_w), out_dtype),
            compiler_params=cp,
            name=name,
        )

    oa = call(
        functools.partial(_sattn_a_body, layer=layer, npages=npages, topk=topk, tk=tk),
        [seq(IDX_HEADS, IDX_DH), seq(IDX_HEADS, 1), seq(1, IDX_DH), seq(A_HEADS, A_DH), seq(1, A_DH), seq(1, A_DH),
         const((tk, tk)), hbm, hbm, hbm],
        A_DH, A_HEADS,
        [pltpu.VMEM((2, IDX_DH, L), F32), pltpu.VMEM((2, A_DH, L), F32), pltpu.VMEM((2, A_DH, L), F32),
         dsem, dsem, dsem],
        "attn_a_sample",
    )(page_table, ts["qi"], ts["wi"], ts["ki"], ts["qa"], ts["ka"], ts["va"], _strict_upper(tk), c_ai, c_ak, c_av)

    lam_rows = [lw[k].reshape(1, B_DH).astype(F32) for k in ("lam_q1", "lam_k1", "lam_q2", "lam_k2")]
    ob = call(
        functools.partial(_sattn_b_body, layer=layer, npages=npages, lam_init=lam_init),
        [const((1, B_DH))] * 4 + [const((1, 2 * B_DH)), seq(2 * B_HEADS, 2 * B_DH), seq(1, 2 * B_DH),
                                    seq(1, 2 * B_DH), hbm, hbm],
        2 * B_DH, B_HEADS,
        [pltpu.VMEM((2, 2 * B_DH, L), F32), pltpu.VMEM((2, L, 2 * B_DH), F32), dsem, dsem],
        "attn_b_sample",
    )(page_table, *lam_rows, lw["g_subln"].reshape(1, 2 * B_DH), ts["qb"], ts["kb"], ts["vb"], c_bk, c_bv)

    olat = call(
        functools.partial(_sattn_c_body, layer=layer, npages=npages),
        [seq(C_HEADS, 2 * C_LAT), seq(1, 2 * C_LAT), hbm, hbm],
        C_LAT, C_HEADS,
        [pltpu.VMEM((2, L, C_LAT), F32), pltpu.VMEM((2, C_ROPE, L), F32), dsem, dsem],
        "attn_c_sample",
    )(page_table, ts["qc"], ts["kcb"], c_lat, c_kr)

    def uv_body(x_ref, w_ref, o_ref):
        o_ref[...] = _dot(x_ref[0].astype(BF16), w_ref[0]).astype(o_ref.dtype)

    oc = pl.pallas_call(
        uv_body,
        grid=(C_HEADS,),
        in_specs=[pl.BlockSpec((1, ns, C_LAT), lambda h: (h, 0, 0)), pl.BlockSpec((1, C_LAT, C_DV), lambda h: (h, 0, 0))],
        out_specs=pl.BlockSpec((ns, C_DV), lambda h: (0, h)),
        out_shape=jax.ShapeDtypeStruct((ns, C_HEADS * C_DV), BF16),
        compiler_params=_cparams(("arbitrary",)),
        name="uv_proj_sample",
    )(jnp.swapaxes(olat, 0, 1), lw["wuv_h"])
    return jnp.concatenate([oa.reshape(ns, -1).astype(BF16), ob.reshape(ns, -1).astype(BF16), oc], axis=-1)


def _mem_attn_body(q_ref, g_ref, k_ref, v_ref, o_ref):
    q = q_ref[0]
    n_mem = k_ref.shape[1] // MEM_HEADS
    scale = MEM_DH ** -0.5 * LOG2E
    for h in range(MEM_HEADS):
        sl = slice(MEM_DH * h, MEM_DH * (h + 1))
        rows = pl.ds(h, n_mem, stride=MEM_HEADS)
        qh = q[:, sl]
        qh = qh * lax.rsqrt(jnp.mean(qh * qh, axis=-1, keepdims=True) + EPS) * g_ref[...] * scale
        s = _dot_t(qh.astype(BF16), k_ref[0, rows, :].astype(BF16))
        p = jnp.exp2(s - jnp.max(s, axis=-1, keepdims=True))
        l = jnp.sum(p, axis=-1, keepdims=True)
        o = _dot(p.astype(BF16), v_ref[0, rows, :].astype(BF16)) * (1.0 / l)
        o_ref[0, :, sl] = o.astype(o_ref.dtype)


def _mem_attn(q3, g_mq, mk, mv, *, tq, kv_layer=None, name):
    G, T, W = q3.shape
    if kv_layer is None:
        kvspec = pl.BlockSpec((1,) + mk.shape[1:], lambda g, i: (g, 0, 0))
    else:
        kvspec = pl.BlockSpec((None, 1) + mk.shape[2:], lambda g, i: (kv_layer, g, 0, 0))
    return pl.pallas_call(
        _mem_attn_body,
        grid=(G, T // tq),
        in_specs=[pl.BlockSpec((1, tq, W), lambda g, i: (g, i, 0)), pl.BlockSpec((1, MEM_DH), lambda g, i: (0, 0)),
                  kvspec, kvspec],
        out_specs=pl.BlockSpec((1, tq, W), lambda g, i: (g, i, 0)),
        out_shape=jax.ShapeDtypeStruct((G, T, W), BF16),
        compiler_params=_cparams(("arbitrary", "arbitrary")),
        name=name,
    )(q3, g_mq.reshape(1, MEM_DH), mk, mv)


def _head_norm_body(x_ref, g_ref, o_ref):
    for h in range(MEM_HEADS):
        sl = slice(MEM_DH * h, MEM_DH * (h + 1))
        xh = x_ref[:, sl]
        o_ref[:, sl] = xh * lax.rsqrt(jnp.mean(xh * xh, axis=-1, keepdims=True) + EPS) * g_ref[...]


def _head_norm(x, g):
    M, W = x.shape
    return pl.pallas_call(
        _head_norm_body,
        grid=(1,),
        in_specs=[pl.BlockSpec((M, W), lambda i: (0, 0)), pl.BlockSpec((1, MEM_DH), lambda i: (0, 0))],
        out_specs=pl.BlockSpec((M, W), lambda i: (0, 0)),
        out_shape=jax.ShapeDtypeStruct((M, W), F32),
        compiler_params=_cparams(("arbitrary",)),
        name="mem_key_norm",
    )(x, g.reshape(1, MEM_DH))


def _permute_w_in(w_in):
    L, D, _ = w_in.shape
    o = 0
    parts = {}
    for name, n in (("qa", 512), ("ka", 64), ("va", 64), ("qi", 512), ("ki", 32), ("wi", 16), ("qb", 512),
                    ("kb", 128), ("vb", 128), ("qc", 768), ("ckv", 128), ("kr", 32)):
        parts[name] = w_in[:, :, o:o + n]
        o += n
    qc = parts["qc"].reshape(L, D, C_HEADS, C_NOPE + C_ROPE)
    pad = lambda a: jnp.pad(a, ((0, 0), (0, 0), (0, LANES - a.shape[-1])))
    out = jnp.concatenate([
        parts["qa"], parts["qb"], parts["qi"],
        qc[..., :C_NOPE].reshape(L, D, C_HEADS * C_NOPE), qc[..., C_NOPE:].reshape(L, D, C_HEADS * C_ROPE),
        pad(parts["ka"]), pad(parts["va"]), parts["kb"], parts["vb"], parts["ckv"],
        pad(parts["ki"]), pad(parts["kr"]), pad(parts["wi"])], axis=-1)
    assert out.shape[-1] == NZ
    return out.astype(BF16)


def _rope_tables(pos, head_dim):
    half = head_dim // 2
    inv = ROPE_THETA ** (-jnp.arange(half, dtype=F32) / half)
    ang = pos.astype(F32)[:, None] * inv[None, :]
    cos, sin = jnp.cos(ang), jnp.sin(ang)
    reps = LANES // head_dim
    return jnp.tile(jnp.concatenate([cos, cos], -1), (1, reps)), jnp.tile(jnp.concatenate([-sin, sin], -1), (1, reps))


def _row_tile(m):
    best = None
    for tm in (640, 512, 384, 256, 128):
        padded = -(-m // tm) * tm
        if best is None or padded < best[1]:
            best = (tm, padded)
    return best


def _col_tile(n, cap):
    return max(t for t in range(LANES, min(n, cap) + 1, LANES) if n % t == 0)


def _tile_gain(g, reps, pad_to=None):
    v = jnp.tile(g.astype(F32), reps)
    if pad_to is not None:
        v = jnp.pad(v, (0, pad_to - v.shape[0]))
    return v.reshape(1, -1)


def kernel(x_prompt, x_sample, mem_prompt, cache_a_k, cache_a_v, cache_a_idx, cache_b_k, cache_b_v, cache_c_lat, cache_c_rope, cache_mem_k, cache_mem_v, page_table, norm_mix, w_in, g_qa, g_ka, g_ki, g_qb, g_kb, lam_q1, lam_k1, lam_q2, lam_k2, g_subln, g_qc, g_kv, g_kr, w_uk, w_uv, w_out, norm_cross, norm_memin, w_mq, w_mk, w_mv, g_mq, g_mk, w_mo, norm_mlp, w_up, w_down):
    nb, S, D = x_prompt.shape
    ns, dec_seq, _ = x_sample.shape
    assert dec_seq == 1
    depth = w_in.shape[0]
    npages = page_table.shape[1]
    n_mem = mem_prompt.shape[1]
    n_p = nb * S
    m_tot = n_p + ns
    tm, m_pad = _row_tile(m_tot)

    w_in_p = _permute_w_in(w_in)
    w_out_b, w_mq_b, w_mo_b = w_out.astype(BF16), w_mq.astype(BF16), w_mo.astype(BF16)
    w_mkv_b = jnp.concatenate([w_mk, w_mv], axis=-1).astype(BF16)
    w_up_b, w_down_b = w_up.astype(BF16), w_down.astype(BF16)
    wuv_h = jnp.swapaxes(w_uv, 1, 2).astype(BF16)
    wuk_t = jnp.transpose(w_uk, (0, 2, 3, 1))
    zeros = jnp.zeros_like(wuk_t[:, 0::2])
    wuk_pairs = jnp.concatenate([jnp.concatenate([wuk_t[:, 0::2], zeros], -1),
                                 jnp.concatenate([zeros, wuk_t[:, 1::2]], -1)], axis=-2).astype(BF16)

    pos = jnp.concatenate([jnp.tile(jnp.arange(S, dtype=jnp.int32), nb),
                           jnp.full((m_pad - n_p,), npages * PAGE_SIZE, jnp.int32)])
    tabs = _rope_tables(pos, A_DH) + _rope_tables(pos, IDX_DH)
    slab = min(-(-ns // LANES) * LANES, m_pad - n_p)
    tabs_s = tuple(t[n_p:n_p + slab] for t in tabs)

    tpage = lambda c: jnp.swapaxes(c, 2, 3)
    cbk = jnp.transpose(cache_b_k, (0, 1, 3, 4, 2)).reshape(cache_b_k.shape[:2] + (2 * B_DH, PAGE_SIZE))
    caches = (tpage(cache_a_k), tpage(cache_a_v), tpage(cache_a_idx), cbk, cache_b_v, cache_c_lat,
              tpage(cache_c_rope))
    mem_w = MEM_HEADS * MEM_DH
    cmk = cache_mem_k.reshape(depth, ns, n_mem * MEM_HEADS, MEM_DH)
    cmv = cache_mem_v.reshape(depth, ns, n_mem * MEM_HEADS, MEM_DH)
    mem_rows = mem_prompt.reshape(nb * n_mem, D)

    x = jnp.concatenate([x_prompt.reshape(n_p, D), x_sample.reshape(ns, D),
                         jnp.zeros((m_pad - m_tot, D), F32)], axis=0)
    tq_mem = min(512, S)
    rows_out = [[] for _ in range(7)]
    mem_out = [[], []]
    for l in range(depth):
        lam_init = 0.8 - 0.6 * math.exp(-0.3 * l)
        lw = {"lam_q1": lam_q1[l], "lam_k1": lam_k1[l], "lam_q2": lam_q2[l], "lam_k2": lam_k2[l],
              "g_subln": g_subln[l].astype(F32), "wuv_h": wuv_h[l]}
        z = _matmul(x, w_in_p[l], g=norm_mix[l], tm=tm, tn=NZ // 2, name="in_proj")
        gains = [_tile_gain(g_qa[l], 2), _tile_gain(g_qb[l], 2), _tile_gain(g_qc[l][:C_NOPE], 2),
                 _tile_gain(g_qc[l][C_NOPE:], 4), _tile_gain(g_ka[l], 1, LANES), _tile_gain(g_kb[l], 2),
                 _tile_gain(g_kv[l], 1), _tile_gain(g_ki[l], 1, LANES), _tile_gain(g_kr[l], 1, LANES)]
        (ka, va, ki, kb, vb, lat, kr, qa, qb, qc, qi, wi, kab, vab, kib, kbb, vbb, kcb) = _prep(
            z, tabs, gains, wuk_pairs[l], tm=tm)
        for lst, a in zip(rows_out, (ka, va, ki, kb, vb, lat, kr)):
            lst.append(a)

        tp = {"qa": qa, "qb": qb, "qc": qc, "qi": qi, "wi": wi, "kab": kab, "vab": vab, "kib": kib,
              "kbb": kbb, "vbb": vbb, "kcb": kcb}
        mix_p = _attn_prompt(tp, lw, lam_init, nb=nb, S=S)
        (ka_s, va_s, ki_s, kb_s, vb_s, _, _, qa_s, qb_s, qc_s, qi_s, wi_s, _, _, _, _, _, kcb_s) = _prep(
            z[n_p:n_p + slab], tabs_s, gains, wuk_pairs[l], tm=slab)
        seq_major = lambda a: jnp.swapaxes(a[:, :ns], 0, 1)
        seq_row = lambda a: a[:ns].reshape(ns, 1, -1)
        ts = {"qa": seq_major(qa_s), "qb": seq_major(qb_s), "qc": seq_major(qc_s), "qi": seq_major(qi_s),
              "wi": wi_s[:ns, :IDX_HEADS].reshape(ns, IDX_HEADS, 1),
              "ki": seq_row(ki_s), "ka": seq_row(ka_s), "va": seq_row(va_s), "kb": seq_row(kb_s),
              "vb": seq_row(vb_s), "kcb": seq_row(kcb_s)}
        mix_s = _attn_sample(ts, lw, caches, page_table, lam_init, l)
        mix = jnp.concatenate([mix_p, mix_s, jnp.zeros((m_pad - m_tot, D), BF16)], axis=0)
        x = _matmul(mix, w_out_b[l], res=x, tm=tm, tn=_col_tile(D, 1024), name="out_proj")

        kv = _matmul(mem_rows, w_mkv_b[l], g=norm_memin[l], tm=mem_rows.shape[0] // nb, tn=mem_w, name="mem_kv")
        mk_p = _head_norm(kv[:, :mem_w], g_mk[l])
        mv_p = kv[:, mem_w:]
        mem_out[0].append(mk_p)
        mem_out[1].append(mv_p)
        qm = _matmul(x, w_mq_b[l], g=norm_cross[l], tm=tm, tn=mem_w, name="mem_q")
        om_p = _mem_attn(qm[:n_p].reshape(nb, S, mem_w), g_mq[l], mk_p.reshape(nb, n_mem * MEM_HEADS, MEM_DH),
                         mv_p.reshape(nb, n_mem * MEM_HEADS, MEM_DH), tq=tq_mem, name="mem_attn_prompt")
        om_s = _mem_attn(qm[n_p:m_tot].reshape(ns, 1, mem_w), g_mq[l], cmk, cmv, tq=1, kv_layer=l,
                         name="mem_attn_sample")
        om = jnp.concatenate([om_p.reshape(n_p, mem_w), om_s.reshape(ns, mem_w),
                              jnp.zeros((m_pad - m_tot, mem_w), BF16)], axis=0)
        x = _matmul(om, w_mo_b[l], res=x, tm=tm, tn=_col_tile(D, 1024), name="mem_out")

        u = _matmul(x, w_up_b[l], g=norm_mlp[l], act="relu2", out_dtype=BF16, tm=tm,
                    tn=_col_tile(w_up.shape[-1], 1024), name="mlp_up")
        x = _matmul(u, w_down_b[l], res=x, tm=tm, tn=_col_tile(D, 512), name="mlp_down")

    def stack_rows(lst, rows, lead, tail):
        return jnp.stack([a[rows].reshape(lead + tail) for a in lst], axis=0)

    tails = [(A_DH,), (A_DH,), (IDX_DH,), (2, B_DH), (2 * B_DH,), (C_LAT,), (C_ROPE,)]
    prow, srow = slice(0, n_p), slice(n_p, m_tot)
    outs_p = [stack_rows(lst, prow, (nb, S), t) for lst, t in zip(rows_out, tails)]
    outs_s = [stack_rows(lst, srow, (ns, 1), t) for lst, t in zip(rows_out, tails)]
    mems = [jnp.stack([a.reshape(nb, n_mem, MEM_HEADS, MEM_DH) for a in lst], axis=0) for lst in mem_out]
    return (x[:n_p].reshape(nb, S, D), x[n_p:m_tot].reshape(ns, 1, D), *outs_p, *mems, *outs_s)
```

```python
import functools
import math

import jax
import jax.numpy as jnp
import numpy as np
from jax import lax
from jax.experimental import pallas as pl
from jax.experimental.pallas import tpu as pltpu

A_HEADS, A_DH = 8, 64
IDX_HEADS, IDX_DH, IDX_TOPK_MAX = 16, 32, 256
B_HEADS, B_DH = 4, 64
C_HEADS, C_NOPE, C_ROPE, C_DV, C_LAT = 8, 64, 32, 128, 128
MEM_HEADS, MEM_DH = 4, 128
PAGE_SIZE = 128
ROPE_THETA = 10000.0
EPS = 1e-6

LANES = 128
SUBLANES = 8
VMEM_CAP_BYTES = 60000 * 1024
LOG2E = 1.4426950408889634
NEG = -1e30
KEY_NINF = -2139095041
INT_MIN = -(2 ** 31)

F32 = jnp.float32
BF16 = jnp.bfloat16

Z_QA, Z_QB, Z_QI, Z_QCN, Z_QCR = 0, 512, 1024, 1536, 2048
Z_KA, Z_VA, Z_KB, Z_VB, Z_CKV, Z_KI, Z_KR, Z_WI = 2304, 2432, 2560, 2688, 2816, 2944, 3072, 3200
NZ = 3328


def _cparams(sem):
    return pltpu.CompilerParams(dimension_semantics=sem, vmem_limit_bytes=VMEM_CAP_BYTES)


def _dot_t(a, b):
    return lax.dot_general(a, b, (((1,), (1,)), ((), ())), preferred_element_type=F32)


def _dot(a, b):
    return jnp.dot(a, b, preferred_element_type=F32)


def _mm_body(*refs, has_norm, act, has_res):
    it = iter(refs)
    x_ref = next(it)
    g_ref = next(it) if has_norm else None
    w_ref = next(it)
    r_ref = next(it) if has_res else None
    o_ref = next(it)
    xn_ref = next(it) if has_norm else None
    if has_norm:
        @pl.when(pl.program_id(1) == 0)
        def _():
            x = x_ref[...].astype(F32)
            ms = jnp.mean(x * x, axis=-1, keepdims=True)
            xn_ref[...] = (x * lax.rsqrt(ms + EPS) * g_ref[...]).astype(BF16)
        xv = xn_ref[...]
    else:
        xv = x_ref[...]
    acc = _dot(xv, w_ref[...])
    if act == "relu2":
        acc = jnp.square(jnp.maximum(acc, 0.0))
    if has_res:
        acc = acc + r_ref[...]
    o_ref[...] = acc.astype(o_ref.dtype)


def _matmul(x, w, *, g=None, res=None, act=None, out_dtype=F32, tm, tn, name):
    M, K = x.shape
    N = w.shape[1]
    assert M % tm == 0 and N % tn == 0, (M, tm, N, tn)
    in_specs = [pl.BlockSpec((tm, K), lambda i, j: (i, 0))]
    args = [x]
    scratch = []
    if g is not None:
        in_specs.append(pl.BlockSpec((1, K), lambda i, j: (0, 0)))
        args.append(g.reshape(1, K).astype(F32))
        scratch.append(pltpu.VMEM((tm, K), BF16))
    in_specs.append(pl.BlockSpec((K, tn), lambda i, j: (0, j)))
    args.append(w)
    if res is not None:
        in_specs.append(pl.BlockSpec((tm, tn), lambda i, j: (i, j)))
        args.append(res)
    body = functools.partial(_mm_body, has_norm=g is not None, act=act, has_res=res is not None)
    return pl.pallas_call(
        body,
        grid=(M // tm, N // tn),
        in_specs=in_specs,
        out_specs=pl.BlockSpec((tm, tn), lambda i, j: (i, j)),
        out_shape=jax.ShapeDtypeStruct((M, N), out_dtype),
        scratch_shapes=scratch,
        compiler_params=_cparams(("arbitrary", "arbitrary")),
        name=name,
    )(*args)


def _lane_iota(width=LANES):
    return lax.broadcasted_iota(jnp.int32, (1, width), 1)


def _seg_sums(col, nseg):
    lane = _lane_iota()
    shift = int(math.log2(LANES // nseg))
    x2 = col * col
    return [jnp.sum(jnp.where((lane >> shift) == s, x2, 0.0), axis=-1, keepdims=True) for s in range(nseg)]


def _seg_select(vals):
    lane = _lane_iota()
    shift = int(math.log2(LANES // len(vals)))
    out = vals[-1]
    for s in range(len(vals) - 2, -1, -1):
        out = jnp.where((lane >> shift) == s, vals[s], out)
    return out


def _rope(y, cos, sin_signed, half):
    lane = _lane_iota()
    first = (lane & (2 * half - 1)) < half
    rot = jnp.where(first, pltpu.roll(y, LANES - half, 1), pltpu.roll(y, half, 1))
    return y * cos + rot * sin_signed


def _prep_body(z_ref, c64_ref, s64_ref, c32_ref, s32_ref,
               gqa_ref, gqb_ref, gqcn_ref, gqcr_ref, gka_ref, gkb_ref, gkv_ref, gki_ref, gkr_ref,
               wuk_ref,
               ka_o, va_o, ki_o, kb_o, vb_o, lat_o, kr_o,
               qa_o, qb_o, qc_o, qi_o, wi_o, kab_o, vab_o, kib_o, kbb_o, vbb_o, kcb_o):
    lane = _lane_iota()
    c64, s64, c32, s32 = c64_ref[...], s64_ref[...], c32_ref[...], s32_ref[...]

    def zcol(off):
        return z_ref[:, off:off + LANES]

    def inv_rms(ss, n):
        return lax.rsqrt(ss * (1.0 / n) + EPS)

    sa = A_DH ** -0.5 * LOG2E
    for j in range(A_HEADS // 2):
        col = zcol(Z_QA + LANES * j)
        inv = _seg_select([inv_rms(s, A_DH) for s in _seg_sums(col, 2)])
        o = _rope(col * inv * gqa_ref[...], c64, s64, A_DH // 2) * sa
        qa_o[2 * j] = o[:, :A_DH].astype(BF16)
        qa_o[2 * j + 1] = pltpu.roll(o, LANES - A_DH, 1)[:, :A_DH].astype(BF16)

    sb = B_DH ** -0.5 * LOG2E
    for h in range(B_HEADS):
        col = zcol(Z_QB + LANES * h)
        inv = _seg_select([inv_rms(s, B_DH) for s in _seg_sums(col, 2)])
        o = _rope(col * inv * gqb_ref[...], c64, s64, B_DH // 2) * sb
        qb_o[h] = jnp.where(lane < B_DH, o, 0.0).astype(BF16)
        qb_o[B_HEADS + h] = jnp.where(lane < B_DH, 0.0, o).astype(BF16)

    per = LANES // IDX_DH
    for j in range(IDX_HEADS // per):
        o = _rope(zcol(Z_QI + LANES * j), c32, s32, IDX_DH // 2)
        for r in range(per):
            piece = o if r == 0 else pltpu.roll(o, LANES - IDX_DH * r, 1)
            qi_o[per * j + r] = piece[:, :IDX_DH].astype(BF16)
    wi_o[...] = zcol(Z_WI) * (IDX_HEADS * IDX_DH) ** -0.5

    sc = (C_NOPE + C_ROPE) ** -0.5 * LOG2E
    ncols = [zcol(Z_QCN + LANES * j) for j in range(C_HEADS // 2)]
    rper = LANES // C_ROPE
    rcols = [zcol(Z_QCR + LANES * j) for j in range(C_HEADS // rper)]
    nsum = [s for c in ncols for s in _seg_sums(c, 2)]
    rsum = [s for c in rcols for s in _seg_sums(c, rper)]
    invs = [inv_rms(nsum[h] + rsum[h], C_NOPE + C_ROPE) for h in range(C_HEADS)]
    for j in range(C_HEADS // 2):
        nn = (ncols[j] * _seg_select(invs[2 * j:2 * j + 2]) * gqcn_ref[...]).astype(BF16)
        qabs = _dot(nn, wuk_ref[j]) * sc
        qc_o[2 * j, :, 0:C_LAT] = qabs[:, :C_LAT].astype(BF16)
        qc_o[2 * j + 1, :, 0:C_LAT] = qabs[:, C_LAT:].astype(BF16)
    for j in range(C_HEADS // rper):
        rn = rcols[j] * _seg_select(invs[rper * j:rper * (j + 1)]) * gqcr_ref[...]
        o = _rope(rn, c32, s32, C_ROPE // 2) * sc
        for r in range(rper):
            piece = o if r == 0 else pltpu.roll(o, LANES - C_ROPE * r, 1)
            qc_o[rper * j + r, :, C_LAT:2 * C_LAT] = jnp.where(lane < C_ROPE, piece, 0.0).astype(BF16)

    col = zcol(Z_KA)
    inv = inv_rms(jnp.sum(col * col, axis=-1, keepdims=True), A_DH)
    o = _rope(col * inv * gka_ref[...], c64, s64, A_DH // 2)[:, :A_DH]
    ka_o[...] = o
    kab_o[...] = o.astype(BF16)
    col = zcol(Z_VA)[:, :A_DH]
    va_o[...] = col
    vab_o[...] = col.astype(BF16)

    col = zcol(Z_KB)
    inv = _seg_select([inv_rms(s, B_DH) for s in _seg_sums(col, 2)])
    o = _rope(col * inv * gkb_ref[...], c64, s64, B_DH // 2)
    kb_o[...] = o
    kbb_o[...] = o.astype(BF16)
    col = zcol(Z_VB)
    vb_o[...] = col
    vbb_o[...] = col.astype(BF16)

    col = zcol(Z_CKV)
    lat = col * inv_rms(jnp.sum(col * col, axis=-1, keepdims=True), C_LAT) * gkv_ref[...]
    lat_o[...] = lat
    kcb_o[:, 0:C_LAT] = lat.astype(BF16)
    col = zcol(Z_KR)
    inv = inv_rms(jnp.sum(col * col, axis=-1, keepdims=True), C_ROPE)
    o = _rope(col * inv * gkr_ref[...], c32, s32, C_ROPE // 2)
    kr_o[...] = o[:, :C_ROPE]
    kcb_o[:, C_LAT:2 * C_LAT] = jnp.where(lane < C_ROPE, o, 0.0).astype(BF16)

    col = zcol(Z_KI)
    inv = inv_rms(jnp.sum(col * col, axis=-1, keepdims=True), IDX_DH)
    o = _rope(col * inv * gki_ref[...], c32, s32, IDX_DH // 2)[:, :IDX_DH]
    ki_o[...] = o
    kib_o[...] = o.astype(BF16)


def _prep(z, tabs, gains, wuk_pairs, *, tm):
    M = z.shape[0]
    row = lambda w: pl.BlockSpec((tm, w), lambda i: (i, 0))
    const = lambda shape: pl.BlockSpec(shape, lambda i: (0,) * len(shape))
    heads = lambda h, w: pl.BlockSpec((h, tm, w), lambda i: (0, i, 0))
    in_specs = [row(NZ)] + [row(LANES)] * 4 + [const((1, LANES))] * 9 + [const(wuk_pairs.shape)]
    out_defs = [
        (A_DH, F32, None), (A_DH, F32, None), (IDX_DH, F32, None), (2 * B_DH, F32, None),
        (2 * B_DH, F32, None), (C_LAT, F32, None), (C_ROPE, F32, None),
        (A_DH, BF16, A_HEADS), (2 * B_DH, BF16, 2 * B_HEADS), (2 * C_LAT, BF16, C_HEADS),
        (IDX_DH, BF16, IDX_HEADS), (LANES, F32, None),
        (A_DH, BF16, None), (A_DH, BF16, None), (IDX_DH, BF16, None), (2 * B_DH, BF16, None),
        (2 * B_DH, BF16, None), (2 * C_LAT, BF16, None),
    ]
    out_specs, out_shape = [], []
    for w, dt, h in out_defs:
        if h is None:
            out_specs.append(row(w))
            out_shape.append(jax.ShapeDtypeStruct((M, w), dt))
        else:
            out_specs.append(heads(h, w))
            out_shape.append(jax.ShapeDtypeStruct((h, M, w), dt))
    return pl.pallas_call(
        _prep_body,
        grid=(M // tm,),
        in_specs=in_specs,
        out_specs=out_specs,
        out_shape=out_shape,
        compiler_params=_cparams(("arbitrary",)),
        name="prep_tokens",
    )(z, *tabs, *gains, wuk_pairs)


def _flash_step(carry, s, v_chunk):
    m, l, acc = carry
    m_new = jnp.maximum(m, jnp.max(s, axis=-1, keepdims=True))
    alpha = jnp.exp2(m - m_new)
    p = jnp.exp2(s - m_new)
    l = alpha * l + jnp.sum(p, axis=-1, keepdims=True)
    acc = alpha * acc + _dot(p.astype(BF16), v_chunk)
    return m_new, l, acc


def _flash_init(rows, dv):
    return (jnp.full((rows, 1), NEG, F32), jnp.zeros((rows, 1), F32), jnp.zeros((rows, dv), F32))


def _causal_bias(i, j, tq, tk):
    qpos = i * tq + lax.broadcasted_iota(jnp.int32, (tq, 1), 0)
    kpos = j * tk + lax.broadcasted_iota(jnp.int32, (1, tk), 1)
    return jnp.where(kpos <= qpos, 0.0, NEG)


def _flash_causal(q, k_ref, v_of, i, *, heads, tq, tk, dv):
    rows = heads * tq

    def chunk(j, carry, bias):
        kc = k_ref[pl.ds(pl.multiple_of(j * tk, tk), tk), :]
        s = _dot_t(q, kc)
        if bias is not None:
            s = (s.reshape(heads, tq, tk) + bias[None]).reshape(rows, tk)
        return _flash_step(carry, s, v_of(j, kc))

    jd = (i * tq) // tk
    carry = lax.fori_loop(0, jd, lambda j, c: chunk(j, c, None), _flash_init(rows, dv))
    _, l, acc = chunk(jd, carry, _causal_bias(i, jd, tq, tk))
    return acc * (1.0 / l)


def _lam_value(lq1, lk1, lq2, lk2, lam_init):
    a = jnp.sum(lq1[...] * lk1[...], axis=-1, keepdims=True)
    b = jnp.sum(lq2[...] * lk2[...], axis=-1, keepdims=True)
    return jnp.exp(a) - jnp.exp(b) + lam_init


def _diff_combine(o, lam, gsub, lam_init):
    outs = []
    for h in range(B_HEADS):
        d = o[h] - lam * o[B_HEADS + h]
        ms = jnp.mean(d * d, axis=-1, keepdims=True)
        outs.append(d * lax.rsqrt(ms + EPS) * gsub * (1.0 - lam_init))
    return outs


def _attn_b_body(mix_ref, lq1, lk1, lq2, lk2, gsub_ref, q_ref, k_ref, v_ref, o_ref, *, tq, tk, lam_init):
    i = pl.program_id(1)
    heads = 2 * B_HEADS
    q = q_ref[...].reshape(heads * tq, 2 * B_DH)

    def v_of(j, kc):
        return v_ref[pl.ds(pl.multiple_of(j * tk, tk), tk), :]

    o = _flash_causal(q, k_ref, v_of, i, heads=heads, tq=tq, tk=tk, dv=2 * B_DH)
    o = o.reshape(heads, tq, 2 * B_DH)
    lam = _lam_value(lq1, lk1, lq2, lk2, lam_init)
    for h, d in enumerate(_diff_combine(o, lam, gsub_ref[...], lam_init)):
        o_ref[:, LANES * h:LANES * (h + 1)] = d.astype(o_ref.dtype)


def _attn_c_body(mix_ref, wuv_ref, q_ref, k_ref, o_ref, *, tq, tk):
    i = pl.program_id(1)
    q = q_ref[...].reshape(C_HEADS * tq, 2 * C_LAT)
    o = _flash_causal(q, k_ref, lambda j, kc: kc[:, :C_LAT], i, heads=C_HEADS, tq=tq, tk=tk, dv=C_LAT)
    o = o.reshape(C_HEADS, tq, C_LAT).astype(BF16)
    for h in range(C_HEADS):
        o_ref[:, C_DV * h:C_DV * (h + 1)] = _dot(o[h], wuv_ref[h]).astype(o_ref.dtype)


def _sortable_key(score):
    b = pltpu.bitcast(score, jnp.int32)
    key = b ^ ((b >> 31) & 0x7FFFFFFF)
    return jnp.where(key == -1, 0, key)


def _attn_a_body(mix_ref, qi_ref, wi_ref, ki_ref, q_ref, k_ref, v_ref, tri_ref, plo_ref, phi_ref, o_ref,
                 key_ref, wb_ref, *, tq, tk, topk):
    i = pl.program_id(1)
    jd = (i * tq) // tk
    nsub = tk // LANES

    w = wi_ref[...]
    for h in range(IDX_HEADS):
        wb_ref[h] = jnp.broadcast_to(w[:, h:h + 1], (tq, LANES))

    def idx_chunk(j, masked):
        kc = ki_ref[pl.ds(pl.multiple_of(j * tk, tk), tk), :]
        score = jnp.zeros((tq, tk), F32)
        for h in range(IDX_HEADS):
            sc = jnp.maximum(_dot_t(qi_ref[h], kc), 0.0)
            score = score + sc * jnp.concatenate([wb_ref[h]] * nsub, axis=-1)
        if masked:
            qpos = i * tq + lax.broadcasted_iota(jnp.int32, (tq, 1), 0)
            kpos = j * tk + lax.broadcasted_iota(jnp.int32, (1, tk), 1)
            score = jnp.where(kpos <= qpos, score, -jnp.inf)
        key_ref[j] = _sortable_key(score)

    def idx_loop(j, c):
        idx_chunk(j, False)
        return c

    lax.fori_loop(0, jd, idx_loop, 0)
    idx_chunk(jd, True)

    def count_ge(cand):
        candb = jnp.broadcast_to(cand, (tq, tk))

        def body(j, acc):
            ge = jnp.where(key_ref[j] >= candb, 1.0, 0.0)
            part = ge[:, 0:LANES]
            for c in range(1, nsub):
                part = part + ge[:, LANES * c:LANES * (c + 1)]
            return acc + part

        acc = lax.fori_loop(0, jd + 1, body, jnp.zeros((tq, LANES), F32))
        return jnp.sum(acc, axis=-1, keepdims=True)

    def bit_body(t, thr):
        cand = thr ^ jnp.left_shift(jnp.int32(1), 31 - t)
        return jnp.where(count_ge(cand) >= topk, cand, thr)

    thr = lax.fori_loop(0, 32, bit_body, jnp.full((tq, 1), INT_MIN, jnp.int32))
    thrb = jnp.broadcast_to(thr, (tq, tk))
    overfull = jnp.sum(jnp.where(count_ge(thr) > topk, 1.0, 0.0))

    q = q_ref[...].reshape(A_HEADS * tq, A_DH)
    rows = A_HEADS * tq

    def attend(j, fc, bias):
        kc = k_ref[pl.ds(pl.multiple_of(j * tk, tk), tk), :]
        vc = v_ref[pl.ds(pl.multiple_of(j * tk, tk), tk), :]
        s = (_dot_t(q, kc).reshape(A_HEADS, tq, tk) + bias[None]).reshape(rows, tk)
        return _flash_step(fc, s, vc)

    def write(l, acc):
        o = (acc * (1.0 / l)).reshape(A_HEADS, tq, A_DH).astype(BF16)
        for j in range(A_HEADS // 2):
            pair = _dot(o[2 * j], plo_ref[...]) + _dot(o[2 * j + 1], phi_ref[...])
            o_ref[:, LANES * j:LANES * (j + 1)] = pair.astype(o_ref.dtype)

    @pl.when(overfull == 0.0)
    def _():
        def att_chunk(j, fc):
            return attend(j, fc, jnp.where(key_ref[j] >= thrb, 0.0, NEG))

        _, l, acc = lax.fori_loop(0, jd + 1, att_chunk, _flash_init(rows, A_DH))
        write(l, acc)

    @pl.when(overfull > 0.0)
    def _():
        need = topk - count_ge(thr + 1)

        def att_chunk(j, carry):
            ties_seen, fc = carry
            kk = key_ref[j]
            tie = jnp.where(kk == thrb, 1.0, 0.0)
            before = _dot(tie.astype(BF16), tri_ref[...]) + ties_seen
            keep_tie = jnp.where(before < need, tie, 0.0)
            keep = jnp.where(kk > thrb, 1.0, keep_tie)
            keep = jnp.where(kk > KEY_NINF, keep, 0.0)
            bias = jnp.where(keep > 0.0, 0.0, NEG)
            return ties_seen + jnp.sum(tie, axis=-1, keepdims=True), attend(j, fc, bias)

        _, (_, l, acc) = lax.fori_loop(0, jd + 1, att_chunk, (jnp.zeros((tq, 1), F32), _flash_init(rows, A_DH)))
        write(l, acc)


def _place_mats():
    eye = np.eye(A_DH, dtype=np.float32)
    zero = np.zeros((A_DH, A_DH), np.float32)
    return (jnp.asarray(np.concatenate([eye, zero], 1), BF16), jnp.asarray(np.concatenate([zero, eye], 1), BF16))


def _strict_upper(n):
    return jnp.asarray(np.triu(np.ones((n, n), np.float32), 1), BF16)


def _prompt_tiles(S):
    tq = min(256, S)
    tk = min(512, S)
    assert S % tq == 0 and S % tk == 0 and tk % tq == 0
    return tq, tk


def _attn_prompt(mix, t, lw, lam_init, *, nb, S):
    tq, tk = _prompt_tiles(S)
    nq = S // tq
    grid = (nb, nq)
    qspec = lambda h, w: pl.BlockSpec((h, tq, w), lambda n, i: (0, n * nq + i, 0))
    kspec = lambda w: pl.BlockSpec((S, w), lambda n, i: (n, 0))
    const = lambda shape: pl.BlockSpec(shape, lambda n, i: (0,) * len(shape))
    ospec = lambda w, cb: pl.BlockSpec((tq, w), lambda n, i: (n * nq + i, cb))
    whole = pl.BlockSpec(memory_space=pl.ANY)
    out_shape = jax.ShapeDtypeStruct(mix.shape, mix.dtype)
    cp = _cparams(("arbitrary", "arbitrary"))
    topk = min(IDX_TOPK_MAX, S // 4)
    assert tk >= topk
    wa, wb, wc = A_HEADS * A_DH, B_HEADS * 2 * B_DH, C_HEADS * C_DV
    assert wa == wb and wc == wa + wb

    plo, phi = _place_mats()
    mix = pl.pallas_call(
        functools.partial(_attn_a_body, tq=tq, tk=tk, topk=topk),
        grid=grid,
        in_specs=[whole, qspec(IDX_HEADS, IDX_DH), pl.BlockSpec((tq, LANES), lambda n, i: (n * nq + i, 0)),
                  kspec(IDX_DH), qspec(A_HEADS, A_DH), kspec(A_DH), kspec(A_DH),
                  const((tk, tk)), const((A_DH, LANES)), const((A_DH, LANES))],
        out_specs=ospec(wa, 0),
        out_shape=out_shape,
        input_output_aliases={0: 0},
        scratch_shapes=[pltpu.VMEM((S // tk, tq, tk), jnp.int32), pltpu.VMEM((IDX_HEADS, tq, LANES), F32)],
        compiler_params=cp,
        name="attn_a_prompt",
    )(mix, t["qi"], t["wi"], t["kib"], t["qa"], t["kab"], t["vab"], _strict_upper(tk), plo, phi)

    lam_rows = [lw[k].reshape(1, B_DH).astype(F32) for k in ("lam_q1", "lam_k1", "lam_q2", "lam_k2")]
    mix = pl.pallas_call(
        functools.partial(_attn_b_body, tq=tq, tk=tk, lam_init=lam_init),
        grid=grid,
        in_specs=[whole] + [const((1, B_DH))] * 4 + [const((1, 2 * B_DH)), qspec(2 * B_HEADS, 2 * B_DH),
                                                       kspec(2 * B_DH), kspec(2 * B_DH)],
        out_specs=ospec(wb, 1),
        out_shape=out_shape,
        input_output_aliases={0: 0},
        compiler_params=cp,
        name="attn_b_prompt",
    )(mix, *lam_rows, lw["g_subln"].reshape(1, 2 * B_DH), t["qb"], t["kbb"], t["vbb"])

    return pl.pallas_call(
        functools.partial(_attn_c_body, tq=tq, tk=tk),
        grid=grid,
        in_specs=[whole, const((C_HEADS, C_LAT, C_DV)), qspec(C_HEADS, 2 * C_LAT), kspec(2 * C_LAT)],
        out_specs=ospec(wc, 1),
        out_shape=out_shape,
        input_output_aliases={0: 0},
        compiler_params=cp,
        name="attn_c_prompt",
    )(mix, lw["wuv_h"], t["qc"], t["kcb"])


def _page_copies(pt_ref, seq, slot, layer, pools, bufs, sems, keys_on_lanes):
    def copies(p):
        pg = pt_ref[seq, p]
        win = pl.ds(pl.multiple_of(p * PAGE_SIZE, PAGE_SIZE), PAGE_SIZE)
        return [pltpu.make_async_copy(pool.at[layer, pg], buf.at[slot, :, win] if t else buf.at[slot, win],
                                      sem.at[slot])
                for pool, buf, sem, t in zip(pools, bufs, sems, keys_on_lanes)]
    return copies


def _paged_fetch(pt_ref, layer, pools, bufs, sems, keys_on_lanes, npages):
    n = pl.program_id(0)
    nseq = pl.num_programs(0)
    slot = lax.rem(n, 2)

    def start(seq, sl):
        cp = _page_copies(pt_ref, seq, sl, layer, pools, bufs, sems, keys_on_lanes)

        def body(p, c):
            for d in cp(p):
                d.start()
            return c
        lax.fori_loop(0, npages, body, 0)

    @pl.when(n == 0)
    def _():
        start(0, 0)

    @pl.when(n + 1 < nseq)
    def _():
        start(n + 1, 1 - slot)

    cp = _page_copies(pt_ref, n, slot, layer, pools, bufs, sems, keys_on_lanes)

    def wbody(p, c):
        for d in cp(p):
            d.wait()
        return c
    lax.fori_loop(0, npages, wbody, 0)
    return slot


def _row_dot(q, krow):
    return jnp.sum(q.astype(F32) * krow.astype(BF16).astype(F32), axis=-1, keepdims=True)


def _softmax_new(s, s_new, v_past, v_new, v_keys_on_lanes=False):
    m = jnp.maximum(jnp.max(s, axis=-1, keepdims=True), s_new)
    p = jnp.exp2(s - m)
    pn = jnp.exp2(s_new - m)
    l = jnp.sum(p, axis=-1, keepdims=True) + pn
    pv = _dot_t(p.astype(BF16), v_past) if v_keys_on_lanes else _dot(p.astype(BF16), v_past)
    o = pv + pn.astype(BF16).astype(F32) * v_new.astype(BF16).astype(F32)
    return o * (1.0 / l)


def _sattn_b_body(pt_ref, lq1, lk1, lq2, lk2, gsub_ref, q_ref, kn_ref, vn_ref, kpool, vpool, o_ref,
                  kbuf, vbuf, ksem, vsem, *, layer, npages, lam_init):
    slot = _paged_fetch(pt_ref, layer, (kpool, vpool), (kbuf, vbuf), (ksem, vsem), (True, False), npages)
    q = q_ref[0]
    s = _dot(q, kbuf[slot].astype(BF16))
    o = _softmax_new(s, _row_dot(q, kn_ref[0]), vbuf[slot].astype(BF16), vn_ref[0])
    lam = _lam_value(lq1, lk1, lq2, lk2, lam_init)
    d = o[0:B_HEADS] - lam * o[B_HEADS:2 * B_HEADS]
    ms = jnp.mean(d * d, axis=-1, keepdims=True)
    o_ref[0] = (d * lax.rsqrt(ms + EPS) * gsub_ref[...] * (1.0 - lam_init)).astype(o_ref.dtype)


def _sattn_c_body(pt_ref, q_ref, kn_ref, lpool, rpool, o_ref, lbuf, rbuf, lsem, rsem, *, layer, npages):
    slot = _paged_fetch(pt_ref, layer, (lpool, rpool), (lbuf, rbuf), (lsem, rsem), (False, True), npages)
    q = q_ref[0]
    lat = lbuf[slot].astype(BF16)
    s = _dot_t(q[:, :C_LAT], lat) + _dot(q[:, C_LAT:C_LAT + C_ROPE], rbuf[slot].astype(BF16))
    kn = kn_ref[0]
    s_new = jnp.sum(q.astype(F32) * kn.astype(F32), axis=-1, keepdims=True)
    o_ref[0] = _softmax_new(s, s_new, lat, kn[:, :C_LAT]).astype(o_ref.dtype)


def _group_fetch(pt_ref, layer, pool, buf, sem, group, npages):
    n = pl.program_id(0)
    nsteps = pl.num_programs(0)
    slot = lax.rem(n, 2)

    def copy(step, sl, g, p):
        pg = pt_ref[step * group + g, p]
        win = pl.ds(pl.multiple_of(p * PAGE_SIZE, PAGE_SIZE), PAGE_SIZE)
        return pltpu.make_async_copy(pool.at[layer, pg], buf.at[sl, g, :, win], sem.at[sl])

    def start(step, sl):
        def body(p, c):
            for g in range(group):
                copy(step, sl, g, p).start()
            return c
        lax.fori_loop(0, npages, body, 0)

    @pl.when(n == 0)
    def _():
        start(0, 0)

    @pl.when(n + 1 < nsteps)
    def _():
        start(n + 1, 1 - slot)

    def wbody(p, c):
        for g in range(group):
            copy(n, slot, g, p).wait()
        return c
    lax.fori_loop(0, npages, wbody, 0)
    return slot


def _sidx_body(pt_ref, qi_ref, wi_ref, kin_ref, tri_ref, ipool, bias_ref, biasn_ref,
               ibuf, isem, keys_ref, newkey_ref, *, layer, npages, topk, tk, group):
    slot = _group_fetch(pt_ref, layer, ipool, ibuf, isem, group, npages)
    L = npages * PAGE_SIZE
    for g in range(group):
        qi = qi_ref[g]
        w = wi_ref[g]
        sc = jnp.maximum(_dot(qi, ibuf[slot, g].astype(BF16)), 0.0)
        keys_ref[g:g + 1, :] = _sortable_key(jnp.sum(sc * w, axis=0, keepdims=True))
        sc_new = jnp.maximum(_row_dot(qi, kin_ref[g]), 0.0)
        key_new = _sortable_key(jnp.sum(sc_new * w, axis=0, keepdims=True))
        newkey_ref[g:g + 1, :] = jnp.broadcast_to(key_new, (1, LANES))
    keys = keys_ref[...]
    key_new = newkey_ref[:, 0:1]

    def count_ge(cand):
        c = jnp.sum(jnp.where(keys >= cand, 1.0, 0.0), axis=-1, keepdims=True)
        return c + jnp.where(key_new >= cand, 1.0, 0.0)

    def bit_body(t, thr):
        cand = thr ^ jnp.left_shift(jnp.int32(1), 31 - t)
        return jnp.where(count_ge(cand) >= topk, cand, thr)

    thr = lax.fori_loop(0, 32, bit_body, jnp.full((group, 1), INT_MIN, jnp.int32))
    bias_ref[...] = jnp.where(keys >= thr, 0.0, NEG)
    biasn_ref[...] = jnp.broadcast_to(jnp.where(key_new >= thr, 0.0, NEG), (group, LANES))
    overfull = jnp.sum(jnp.where(count_ge(thr) > topk, 1.0, 0.0))

    @pl.when(overfull > 0.0)
    def _():
        need = topk - count_ge(thr + 1)
        ties_seen = jnp.zeros((group, 1), F32)
        for c in range(L // tk):
            kk = keys_ref[:, c * tk:(c + 1) * tk]
            tie = jnp.where(kk == thr, 1.0, 0.0)
            before = _dot(tie.astype(BF16), tri_ref[...]) + ties_seen
            keep = jnp.where(kk > thr, 1.0, jnp.where(before < need, tie, 0.0))
            bias_ref[:, c * tk:(c + 1) * tk] = jnp.where(keep > 0.0, 0.0, NEG)
            ties_seen = ties_seen + jnp.sum(tie, axis=-1, keepdims=True)
        keep_new = jnp.where(key_new > thr, 1.0,
                             jnp.where(key_new == thr, jnp.where(ties_seen < need, 1.0, 0.0), 0.0))
        biasn_ref[...] = jnp.broadcast_to(jnp.where(keep_new > 0.0, 0.0, NEG), (group, LANES))


def _sattn_a_body(pt_ref, q_ref, kn_ref, vn_ref, bias_ref, biasn_ref, kpool, vpool, o_ref,
                  kbuf, vbuf, ksem, vsem, *, layer, npages):
    slot = _paged_fetch(pt_ref, layer, (kpool, vpool), (kbuf, vbuf), (ksem, vsem), (True, True), npages)
    q = q_ref[0]
    s = _dot(q, kbuf[slot].astype(BF16)) + bias_ref[0]
    s_new = _row_dot(q, kn_ref[0]) + biasn_ref[0][:, 0:1]
    o_ref[0] = _softmax_new(s, s_new, vbuf[slot].astype(BF16), vn_ref[0], True).astype(o_ref.dtype)


def _attn_sample(ts, lw, caches, page_table, lam_init, layer):
    c_ak, c_av, c_ai, c_bk, c_bv, c_lat, c_kr = caches
    ns, npages = page_table.shape
    L = npages * PAGE_SIZE
    topk = min(IDX_TOPK_MAX, (L + 1) // 4)
    tk = min(512, L)
    assert L % tk == 0
    seq = lambda a, b: pl.BlockSpec((1, a, b), lambda n, pt: (n, 0, 0))
    const = lambda shape: pl.BlockSpec(shape, lambda n, pt: (0,) * len(shape))
    hbm = pl.BlockSpec(memory_space=pl.ANY)
    cp = _cparams(("arbitrary",))
    dsem = pltpu.SemaphoreType.DMA((2,))

    def call(body, in_specs, out_w, out_h, scratch, name, out_dtype=F32):
        return pl.pallas_call(
            body,
            grid_spec=pltpu.PrefetchScalarGridSpec(
                num_scalar_prefetch=1, grid=(ns,), in_specs=in_specs,
                out_specs=seq(out_h, out_w), scratch_shapes=scratch),
            out_shape=jax.ShapeDtypeStruct((ns, out_h, out_w), out_dtype),
            compiler_params=cp,
            name=name,
        )

    group = SUBLANES if ns % SUBLANES == 0 else ns
    grp = lambda a, b: pl.BlockSpec((group, a, b), lambda n, pt: (n, 0, 0))
    grow = lambda w: pl.BlockSpec((group, w), lambda n, pt: (n, 0))
    bias, bias_new = pl.pallas_call(
        functools.partial(_sidx_body, layer=layer, npages=npages, topk=topk, tk=tk, group=group),
        grid_spec=pltpu.PrefetchScalarGridSpec(
            num_scalar_prefetch=1, grid=(ns // group,),
            in_specs=[grp(IDX_HEADS, IDX_DH), grp(IDX_HEADS, 1), grp(1, IDX_DH), const((tk, tk)), hbm],
            out_specs=[grow(L), grow(LANES)],
            scratch_shapes=[pltpu.VMEM((2, group, IDX_DH, L), F32), dsem,
                            pltpu.VMEM((group, L), jnp.int32), pltpu.VMEM((group, LANES), jnp.int32)]),
        out_shape=[jax.ShapeDtypeStruct((ns, L), F32), jax.ShapeDtypeStruct((ns, LANES), F32)],
        compiler_params=cp,
        name="select_a_sample",
    )(page_table, ts["qi"], ts["wi"], ts["ki"], _strict_upper(tk), c_ai)

    oa = call(
        functools.partial(_sattn_a_body, layer=layer, npages=npages),
        [seq(A_HEADS, A_DH), seq(1, A_DH), seq(1, A_DH), seq(1, L), seq(1, LANES), hbm, hbm],
        A_DH, A_HEADS,
        [pltpu.VMEM((2, A_DH, L), F32), pltpu.VMEM((2, A_DH, L), F32), dsem, dsem],
        "attn_a_sample",
    )(page_table, ts["qa"], ts["ka"], ts["va"], bias.reshape(ns, 1, L), bias_new.reshape(ns, 1, LANES), c_ak, c_av)

    lam_rows = [lw[k].reshape(1, B_DH).astype(F32) for k in ("lam_q1", "lam_k1", "lam_q2", "lam_k2")]
    ob = call(
        functools.partial(_sattn_b_body, layer=layer, npages=npages, lam_init=lam_init),
        [const((1, B_DH))] * 4 + [const((1, 2 * B_DH)), seq(2 * B_HEADS, 2 * B_DH), seq(1, 2 * B_DH),
                                    seq(1, 2 * B_DH), hbm, hbm],
        2 * B_DH, B_HEADS,
        [pltpu.VMEM((2, 2 * B_DH, L), F32), pltpu.VMEM((2, L, 2 * B_DH), F32), dsem, dsem],
        "attn_b_sample",
    )(page_table, *lam_rows, lw["g_subln"].reshape(1, 2 * B_DH), ts["qb"], ts["kb"], ts["vb"], c_bk, c_bv)

    olat = call(
        functools.partial(_sattn_c_body, layer=layer, npages=npages),
        [seq(C_HEADS, 2 * C_LAT), seq(1, 2 * C_LAT), hbm, hbm],
        C_LAT, C_HEADS,
        [pltpu.VMEM((2, L, C_LAT), F32), pltpu.VMEM((2, C_ROPE, L), F32), dsem, dsem],
        "attn_c_sample",
    )(page_table, ts["qc"], ts["kcb"], c_lat, c_kr)

    def uv_body(x_ref, w_ref, o_ref):
        o_ref[...] = _dot(x_ref[0].astype(BF16), w_ref[0]).astype(o_ref.dtype)

    oc = pl.pallas_call(
        uv_body,
        grid=(C_HEADS,),
        in_specs=[pl.BlockSpec((1, ns, C_LAT), lambda h: (h, 0, 0)), pl.BlockSpec((1, C_LAT, C_DV), lambda h: (h, 0, 0))],
        out_specs=pl.BlockSpec((ns, C_DV), lambda h: (0, h)),
        out_shape=jax.ShapeDtypeStruct((ns, C_HEADS * C_DV), BF16),
        compiler_params=_cparams(("arbitrary",)),
        name="uv_proj_sample",
    )(jnp.swapaxes(olat, 0, 1), lw["wuv_h"])
    return jnp.concatenate([oa.reshape(ns, -1).astype(BF16), ob.reshape(ns, -1).astype(BF16), oc], axis=-1)


def _mem_attn_body(q_ref, g_ref, k_ref, v_ref, o_ref):
    n_mem = k_ref.shape[1] // MEM_HEADS
    scale = MEM_DH ** -0.5 * LOG2E
    for b in range(q_ref.shape[0]):
        q = q_ref[b]
        for h in range(MEM_HEADS):
            sl = slice(MEM_DH * h, MEM_DH * (h + 1))
            rows = pl.ds(h, n_mem, stride=MEM_HEADS)
            qh = q[:, sl]
            qh = qh * lax.rsqrt(jnp.mean(qh * qh, axis=-1, keepdims=True) + EPS) * g_ref[...] * scale
            s = _dot_t(qh.astype(BF16), k_ref[b, rows, :].astype(BF16))
            p = jnp.exp2(s - jnp.max(s, axis=-1, keepdims=True))
            l = jnp.sum(p, axis=-1, keepdims=True)
            o = _dot(p.astype(BF16), v_ref[b, rows, :].astype(BF16)) * (1.0 / l)
            o_ref[b, :, sl] = o.astype(o_ref.dtype)


def _mem_attn(q3, g_mq, mk, mv, *, tq, gb=1, kv_layer=None, name):
    G, T, W = q3.shape
    assert G % gb == 0
    if kv_layer is None:
        kvspec = pl.BlockSpec((gb,) + mk.shape[1:], lambda g, i: (g, 0, 0))
    else:
        kvspec = pl.BlockSpec((None, gb) + mk.shape[2:], lambda g, i: (kv_layer, g, 0, 0))
    return pl.pallas_call(
        _mem_attn_body,
        grid=(G // gb, T // tq),
        in_specs=[pl.BlockSpec((gb, tq, W), lambda g, i: (g, i, 0)), pl.BlockSpec((1, MEM_DH), lambda g, i: (0, 0)),
                  kvspec, kvspec],
        out_specs=pl.BlockSpec((gb, tq, W), lambda g, i: (g, i, 0)),
        out_shape=jax.ShapeDtypeStruct((G, T, W), BF16),
        compiler_params=_cparams(("arbitrary", "arbitrary")),
        name=name,
    )(q3, g_mq.reshape(1, MEM_DH), mk, mv)


def _head_norm_body(x_ref, g_ref, o_ref):
    for h in range(MEM_HEADS):
        sl = slice(MEM_DH * h, MEM_DH * (h + 1))
        xh = x_ref[:, sl]
        o_ref[:, sl] = xh * lax.rsqrt(jnp.mean(xh * xh, axis=-1, keepdims=True) + EPS) * g_ref[...]


def _head_norm(x, g):
    M, W = x.shape
    return pl.pallas_call(
        _head_norm_body,
        grid=(1,),
        in_specs=[pl.BlockSpec((M, W), lambda i: (0, 0)), pl.BlockSpec((1, MEM_DH), lambda i: (0, 0))],
        out_specs=pl.BlockSpec((M, W), lambda i: (0, 0)),
        out_shape=jax.ShapeDtypeStruct((M, W), F32),
        compiler_params=_cparams(("arbitrary",)),
        name="mem_key_norm",
    )(x, g.reshape(1, MEM_DH))


def _permute_w_in(w_in):
    L, D, _ = w_in.shape
    o = 0
    parts = {}
    for name, n in (("qa", 512), ("ka", 64), ("va", 64), ("qi", 512), ("ki", 32), ("wi", 16), ("qb", 512),
                    ("kb", 128), ("vb", 128), ("qc", 768), ("ckv", 128), ("kr", 32)):
        parts[name] = w_in[:, :, o:o + n]
        o += n
    qc = parts["qc"].reshape(L, D, C_HEADS, C_NOPE + C_ROPE)
    pad = lambda a: jnp.pad(a, ((0, 0), (0, 0), (0, LANES - a.shape[-1])))
    out = jnp.concatenate([
        parts["qa"], parts["qb"], parts["qi"],
        qc[..., :C_NOPE].reshape(L, D, C_HEADS * C_NOPE), qc[..., C_NOPE:].reshape(L, D, C_HEADS * C_ROPE),
        pad(parts["ka"]), pad(parts["va"]), parts["kb"], parts["vb"], parts["ckv"],
        pad(parts["ki"]), pad(parts["kr"]), pad(parts["wi"])], axis=-1)
    assert out.shape[-1] == NZ
    return out.astype(BF16)


def _rope_tables(pos, head_dim):
    half = head_dim // 2
    inv = ROPE_THETA ** (-jnp.arange(half, dtype=F32) / half)
    ang = pos.astype(F32)[:, None] * inv[None, :]
    cos, sin = jnp.cos(ang), jnp.sin(ang)
    reps = LANES // head_dim
    return jnp.tile(jnp.concatenate([cos, cos], -1), (1, reps)), jnp.tile(jnp.concatenate([-sin, sin], -1), (1, reps))


def _row_tile(m):
    best = None
    for tm in (640, 512, 384, 256, 128):
        padded = -(-m // tm) * tm
        if best is None or padded < best[1]:
            best = (tm, padded)
    return best


def _col_tile(n, cap):
    return max(t for t in range(LANES, min(n, cap) + 1, LANES) if n % t == 0)


def _tile_gain(g, reps, pad_to=None):
    v = jnp.tile(g.astype(F32), reps)
    if pad_to is not None:
        v = jnp.pad(v, (0, pad_to - v.shape[0]))
    return v.reshape(1, -1)


def kernel(x_prompt, x_sample, mem_prompt, cache_a_k, cache_a_v, cache_a_idx, cache_b_k, cache_b_v, cache_c_lat, cache_c_rope, cache_mem_k, cache_mem_v, page_table, norm_mix, w_in, g_qa, g_ka, g_ki, g_qb, g_kb, lam_q1, lam_k1, lam_q2, lam_k2, g_subln, g_qc, g_kv, g_kr, w_uk, w_uv, w_out, norm_cross, norm_memin, w_mq, w_mk, w_mv, g_mq, g_mk, w_mo, norm_mlp, w_up, w_down):
    nb, S, D = x_prompt.shape
    ns, dec_seq, _ = x_sample.shape
    assert dec_seq == 1
    depth = w_in.shape[0]
    npages = page_table.shape[1]
    n_mem = mem_prompt.shape[1]
    n_p = nb * S
    m_tot = n_p + ns
    tm, m_pad = _row_tile(m_tot)

    w_in_p = _permute_w_in(w_in)
    w_out_b, w_mq_b, w_mo_b = w_out.astype(BF16), w_mq.astype(BF16), w_mo.astype(BF16)
    w_mkv_b = jnp.concatenate([w_mk, w_mv], axis=-1).astype(BF16)
    w_up_b, w_down_b = w_up.astype(BF16), w_down.astype(BF16)
    wuv_h = jnp.swapaxes(w_uv, 1, 2).astype(BF16)
    wuk_t = jnp.transpose(w_uk, (0, 2, 3, 1))
    zeros = jnp.zeros_like(wuk_t[:, 0::2])
    wuk_pairs = jnp.concatenate([jnp.concatenate([wuk_t[:, 0::2], zeros], -1),
                                 jnp.concatenate([zeros, wuk_t[:, 1::2]], -1)], axis=-2).astype(BF16)

    pos = jnp.concatenate([jnp.tile(jnp.arange(S, dtype=jnp.int32), nb),
                           jnp.full((m_pad - n_p,), npages * PAGE_SIZE, jnp.int32)])
    tabs = _rope_tables(pos, A_DH) + _rope_tables(pos, IDX_DH)
    slab = min(-(-ns // LANES) * LANES, m_pad - n_p)
    tabs_s = tuple(t[n_p:n_p + slab] for t in tabs)

    tpage = lambda c: jnp.swapaxes(c, 2, 3)
    cbk = jnp.transpose(cache_b_k, (0, 1, 3, 4, 2)).reshape(cache_b_k.shape[:2] + (2 * B_DH, PAGE_SIZE))
    caches = (tpage(cache_a_k), tpage(cache_a_v), tpage(cache_a_idx), cbk, cache_b_v, cache_c_lat,
              tpage(cache_c_rope))
    mem_w = MEM_HEADS * MEM_DH
    cmk = cache_mem_k.reshape(depth, ns, n_mem * MEM_HEADS, MEM_DH)
    cmv = cache_mem_v.reshape(depth, ns, n_mem * MEM_HEADS, MEM_DH)
    mem_rows = mem_prompt.reshape(nb * n_mem, D)

    x = jnp.concatenate([x_prompt.reshape(n_p, D), x_sample.reshape(ns, D),
                         jnp.zeros((m_pad - m_tot, D), F32)], axis=0)
    tq_mem = min(512, S)
    rows_out = [[] for _ in range(7)]
    mem_out = [[], []]
    for l in range(depth):
        lam_init = 0.8 - 0.6 * math.exp(-0.3 * l)
        lw = {"lam_q1": lam_q1[l], "lam_k1": lam_k1[l], "lam_q2": lam_q2[l], "lam_k2": lam_k2[l],
              "g_subln": g_subln[l].astype(F32), "wuv_h": wuv_h[l]}
        z = _matmul(x, w_in_p[l], g=norm_mix[l], tm=tm, tn=NZ // 2, name="in_proj")
        gains = [_tile_gain(g_qa[l], 2), _tile_gain(g_qb[l], 2), _tile_gain(g_qc[l][:C_NOPE], 2),
                 _tile_gain(g_qc[l][C_NOPE:], 4), _tile_gain(g_ka[l], 1, LANES), _tile_gain(g_kb[l], 2),
                 _tile_gain(g_kv[l], 1), _tile_gain(g_ki[l], 1, LANES), _tile_gain(g_kr[l], 1, LANES)]
        (ka, va, ki, kb, vb, lat, kr, qa, qb, qc, qi, wi, kab, vab, kib, kbb, vbb, kcb) = _prep(
            z, tabs, gains, wuk_pairs[l], tm=tm)
        for lst, a in zip(rows_out, (ka, va, ki, kb, vb, lat, kr)):
            lst.append(a)

        tp = {"qa": qa, "qb": qb, "qc": qc, "qi": qi, "wi": wi, "kab": kab, "vab": vab, "kib": kib,
              "kbb": kbb, "vbb": vbb, "kcb": kcb}
        mix = _attn_prompt(jnp.zeros((m_pad, D), BF16), tp, lw, lam_init, nb=nb, S=S)
        (ka_s, va_s, ki_s, kb_s, vb_s, _, _, qa_s, qb_s, qc_s, qi_s, wi_s, _, _, _, _, _, kcb_s) = _prep(
            z[n_p:n_p + slab], tabs_s, gains, wuk_pairs[l], tm=slab)
        seq_major = lambda a: jnp.swapaxes(a[:, :ns], 0, 1)
        seq_row = lambda a: a[:ns].reshape(ns, 1, -1)
        ts = {"qa": seq_major(qa_s), "qb": seq_major(qb_s), "qc": seq_major(qc_s), "qi": seq_major(qi_s),
              "wi": wi_s[:ns, :IDX_HEADS].reshape(ns, IDX_HEADS, 1),
              "ki": seq_row(ki_s), "ka": seq_row(ka_s), "va": seq_row(va_s), "kb": seq_row(kb_s),
              "vb": seq_row(vb_s), "kcb": seq_row(kcb_s)}
        mix_s = _attn_sample(ts, lw, caches, page_table, lam_init, l)
        mix = lax.dynamic_update_slice(mix, mix_s, (n_p, 0))
        x = _matmul(mix, w_out_b[l], res=x, tm=tm, tn=_col_tile(D, 1024), name="out_proj")

        kv = _matmul(mem_rows, w_mkv_b[l], g=norm_memin[l], tm=mem_rows.shape[0] // nb, tn=mem_w, name="mem_kv")
        mk_p = _head_norm(kv[:, :mem_w], g_mk[l])
        mv_p = kv[:, mem_w:]
        mem_out[0].append(mk_p)
        mem_out[1].append(mv_p)
        qm = _matmul(x, w_mq_b[l], g=norm_cross[l], tm=tm, tn=mem_w, name="mem_q")
        om_p = _mem_attn(qm[:n_p].reshape(nb, S, mem_w), g_mq[l], mk_p.reshape(nb, n_mem * MEM_HEADS, MEM_DH),
                         mv_p.reshape(nb, n_mem * MEM_HEADS, MEM_DH), tq=tq_mem, name="mem_attn_prompt")
        om_s = _mem_attn(qm[n_p:m_tot].reshape(ns, 1, mem_w), g_mq[l], cmk, cmv, tq=1,
                         gb=SUBLANES if ns % SUBLANES == 0 else 1, kv_layer=l, name="mem_attn_sample")
        om = jnp.concatenate([om_p.reshape(n_p, mem_w), om_s.reshape(ns, mem_w),
                              jnp.zeros((m_pad - m_tot, mem_w), BF16)], axis=0)
        x = _matmul(om, w_mo_b[l], res=x, tm=tm, tn=_col_tile(D, 1024), name="mem_out")

        u = _matmul(x, w_up_b[l], g=norm_mlp[l], act="relu2", out_dtype=BF16, tm=tm,
                    tn=_col_tile(w_up.shape[-1], 1024), name="mlp_up")
        x = _matmul(u, w_down_b[l], res=x, tm=tm, tn=_col_tile(D, 512), name="mlp_down")

    def stack_rows(lst, rows, lead, tail):
        return jnp.stack([a[rows].reshape(lead + tail) for a in lst], axis=0)

    tails = [(A_DH,), (A_DH,), (IDX_DH,), (2, B_DH), (2 * B_DH,), (C_LAT,), (C_ROPE,)]
    prow, srow = slice(0, n_p), slice(n_p, m_tot)
    outs_p = [stack_rows(lst, prow, (nb, S), t) for lst, t in zip(rows_out, tails)]
    outs_s = [stack_rows(lst, srow, (ns, 1), t) for lst, t in zip(rows_out, tails)]
    mems = [jnp.stack([a.reshape(nb, n_mem, MEM_HEADS, MEM_DH) for a in lst], axis=0) for lst in mem_out]
    return (x[:n_p].reshape(nb, S, D), x[n_p:m_tot].reshape(ns, 1, D), *outs_p, *mems, *outs_s)
```

```python
import functools
import math

import jax
import jax.numpy as jnp
import numpy as np
from jax import lax
from jax.experimental import pallas as pl
from jax.experimental.pallas import tpu as pltpu

A_HEADS, A_DH = 8, 64
IDX_HEADS, IDX_DH, IDX_TOPK_MAX = 16, 32, 256
B_HEADS, B_DH = 4, 64
C_HEADS, C_NOPE, C_ROPE, C_DV, C_LAT = 8, 64, 32, 128, 128
MEM_HEADS, MEM_DH = 4, 128
PAGE_SIZE = 128
ROPE_THETA = 10000.0
EPS = 1e-6

LANES = 128
SUBLANES = 8
VMEM_CAP_BYTES = 60000 * 1024
LOG2E = 1.4426950408889634
NEG = -1e30
KEY_NINF = -2139095041
INT_MIN = -(2 ** 31)

F32 = jnp.float32
BF16 = jnp.bfloat16

Z_QA, Z_QB, Z_QI, Z_QCN, Z_QCR = 0, 512, 1024, 1536, 2048
Z_KA, Z_VA, Z_KB, Z_VB, Z_CKV, Z_KI, Z_KR, Z_WI = 2304, 2432, 2560, 2688, 2816, 2944, 3072, 3200
NZ = 3328


def _cparams(sem):
    return pltpu.CompilerParams(dimension_semantics=sem, vmem_limit_bytes=VMEM_CAP_BYTES)


def _dot_t(a, b):
    return lax.dot_general(a, b, (((1,), (1,)), ((), ())), preferred_element_type=F32)


def _dot(a, b):
    return jnp.dot(a, b, preferred_element_type=F32)


def _mm_body(*refs, has_norm, act, has_res):
    it = iter(refs)
    x_ref = next(it)
    g_ref = next(it) if has_norm else None
    w_ref = next(it)
    r_ref = next(it) if has_res else None
    o_ref = next(it)
    xn_ref = next(it) if has_norm else None
    if has_norm:
        @pl.when(pl.program_id(1) == 0)
        def _():
            x = x_ref[...].astype(F32)
            ms = jnp.mean(x * x, axis=-1, keepdims=True)
            xn_ref[...] = (x * lax.rsqrt(ms + EPS) * g_ref[...]).astype(BF16)
        xv = xn_ref[...]
    else:
        xv = x_ref[...]
    acc = _dot(xv, w_ref[...])
    if act == "relu2":
        acc = jnp.square(jnp.maximum(acc, 0.0))
    if has_res:
        acc = acc + r_ref[...]
    o_ref[...] = acc.astype(o_ref.dtype)


def _matmul(x, w, *, g=None, res=None, act=None, out_dtype=F32, tm, tn, name):
    M, K = x.shape
    N = w.shape[1]
    assert M % tm == 0 and N % tn == 0, (M, tm, N, tn)
    in_specs = [pl.BlockSpec((tm, K), lambda i, j: (i, 0))]
    args = [x]
    scratch = []
    if g is not None:
        in_specs.append(pl.BlockSpec((1, K), lambda i, j: (0, 0)))
        args.append(g.reshape(1, K).astype(F32))
        scratch.append(pltpu.VMEM((tm, K), BF16))
    in_specs.append(pl.BlockSpec((K, tn), lambda i, j: (0, j)))
    args.append(w)
    if res is not None:
        in_specs.append(pl.BlockSpec((tm, tn), lambda i, j: (i, j)))
        args.append(res)
    body = functools.partial(_mm_body, has_norm=g is not None, act=act, has_res=res is not None)
    return pl.pallas_call(
        body,
        grid=(M // tm, N // tn),
        in_specs=in_specs,
        out_specs=pl.BlockSpec((tm, tn), lambda i, j: (i, j)),
        out_shape=jax.ShapeDtypeStruct((M, N), out_dtype),
        scratch_shapes=scratch,
        compiler_params=_cparams(("arbitrary", "arbitrary")),
        name=name,
    )(*args)


def _lane_iota(width=LANES):
    return lax.broadcasted_iota(jnp.int32, (1, width), 1)


def _seg_sums(col, nseg):
    lane = _lane_iota()
    shift = int(math.log2(LANES // nseg))
    x2 = col * col
    return [jnp.sum(jnp.where((lane >> shift) == s, x2, 0.0), axis=-1, keepdims=True) for s in range(nseg)]


def _seg_select(vals):
    lane = _lane_iota()
    shift = int(math.log2(LANES // len(vals)))
    out = vals[-1]
    for s in range(len(vals) - 2, -1, -1):
        out = jnp.where((lane >> shift) == s, vals[s], out)
    return out


def _rope(y, cos, sin_signed, half):
    lane = _lane_iota()
    first = (lane & (2 * half - 1)) < half
    rot = jnp.where(first, pltpu.roll(y, LANES - half, 1), pltpu.roll(y, half, 1))
    return y * cos + rot * sin_signed


def _prep_body(z_ref, c64_ref, s64_ref, c32_ref, s32_ref,
               gqa_ref, gqb_ref, gqcn_ref, gqcr_ref, gka_ref, gkb_ref, gkv_ref, gki_ref, gkr_ref,
               wuk_ref,
               ka_o, va_o, ki_o, kb_o, vb_o, lat_o, kr_o,
               qa_o, qb_o, qc_o, qi_o, wi_o, kab_o, vab_o, kib_o, kbb_o, vbb_o, kcb_o):
    lane = _lane_iota()
    c64, s64, c32, s32 = c64_ref[...], s64_ref[...], c32_ref[...], s32_ref[...]

    def zcol(off):
        return z_ref[:, off:off + LANES]

    def inv_rms(ss, n):
        return lax.rsqrt(ss * (1.0 / n) + EPS)

    sa = A_DH ** -0.5 * LOG2E
    for j in range(A_HEADS // 2):
        col = zcol(Z_QA + LANES * j)
        inv = _seg_select([inv_rms(s, A_DH) for s in _seg_sums(col, 2)])
        o = _rope(col * inv * gqa_ref[...], c64, s64, A_DH // 2) * sa
        qa_o[2 * j] = o[:, :A_DH].astype(BF16)
        qa_o[2 * j + 1] = pltpu.roll(o, LANES - A_DH, 1)[:, :A_DH].astype(BF16)

    sb = B_DH ** -0.5 * LOG2E
    for h in range(B_HEADS):
        col = zcol(Z_QB + LANES * h)
        inv = _seg_select([inv_rms(s, B_DH) for s in _seg_sums(col, 2)])
        o = _rope(col * inv * gqb_ref[...], c64, s64, B_DH // 2) * sb
        qb_o[h] = jnp.where(lane < B_DH, o, 0.0).astype(BF16)
        qb_o[B_HEADS + h] = jnp.where(lane < B_DH, 0.0, o).astype(BF16)

    per = LANES // IDX_DH
    for j in range(IDX_HEADS // per):
        o = _rope(zcol(Z_QI + LANES * j), c32, s32, IDX_DH // 2)
        for r in range(per):
            piece = o if r == 0 else pltpu.roll(o, LANES - IDX_DH * r, 1)
            qi_o[per * j + r] = piece[:, :IDX_DH].astype(BF16)
    wi_o[...] = zcol(Z_WI) * (IDX_HEADS * IDX_DH) ** -0.5

    sc = (C_NOPE + C_ROPE) ** -0.5 * LOG2E
    ncols = [zcol(Z_QCN + LANES * j) for j in range(C_HEADS // 2)]
    rper = LANES // C_ROPE
    rcols = [zcol(Z_QCR + LANES * j) for j in range(C_HEADS // rper)]
    nsum = [s for c in ncols for s in _seg_sums(c, 2)]
    rsum = [s for c in rcols for s in _seg_sums(c, rper)]
    invs = [inv_rms(nsum[h] + rsum[h], C_NOPE + C_ROPE) for h in range(C_HEADS)]
    for j in range(C_HEADS // 2):
        nn = (ncols[j] * _seg_select(invs[2 * j:2 * j + 2]) * gqcn_ref[...]).astype(BF16)
        qabs = _dot(nn, wuk_ref[j]) * sc
        qc_o[2 * j, :, 0:C_LAT] = qabs[:, :C_LAT].astype(BF16)
        qc_o[2 * j + 1, :, 0:C_LAT] = qabs[:, C_LAT:].astype(BF16)
    for j in range(C_HEADS // rper):
        rn = rcols[j] * _seg_select(invs[rper * j:rper * (j + 1)]) * gqcr_ref[...]
        o = _rope(rn, c32, s32, C_ROPE // 2) * sc
        for r in range(rper):
            piece = o if r == 0 else pltpu.roll(o, LANES - C_ROPE * r, 1)
            qc_o[rper * j + r, :, C_LAT:2 * C_LAT] = jnp.where(lane < C_ROPE, piece, 0.0).astype(BF16)

    col = zcol(Z_KA)
    inv = inv_rms(jnp.sum(col * col, axis=-1, keepdims=True), A_DH)
    o = _rope(col * inv * gka_ref[...], c64, s64, A_DH // 2)[:, :A_DH]
    ka_o[...] = o
    kab_o[...] = o.astype(BF16)
    col = zcol(Z_VA)[:, :A_DH]
    va_o[...] = col
    vab_o[...] = col.astype(BF16)

    col = zcol(Z_KB)
    inv = _seg_select([inv_rms(s, B_DH) for s in _seg_sums(col, 2)])
    o = _rope(col * inv * gkb_ref[...], c64, s64, B_DH // 2)
    kb_o[...] = o
    kbb_o[...] = o.astype(BF16)
    col = zcol(Z_VB)
    vb_o[...] = col
    vbb_o[...] = col.astype(BF16)

    col = zcol(Z_CKV)
    lat = col * inv_rms(jnp.sum(col * col, axis=-1, keepdims=True), C_LAT) * gkv_ref[...]
    lat_o[...] = lat
    kcb_o[:, 0:C_LAT] = lat.astype(BF16)
    col = zcol(Z_KR)
    inv = inv_rms(jnp.sum(col * col, axis=-1, keepdims=True), C_ROPE)
    o = _rope(col * inv * gkr_ref[...], c32, s32, C_ROPE // 2)
    kr_o[...] = o[:, :C_ROPE]
    kcb_o[:, C_LAT:2 * C_LAT] = jnp.where(lane < C_ROPE, o, 0.0).astype(BF16)

    col = zcol(Z_KI)
    inv = inv_rms(jnp.sum(col * col, axis=-1, keepdims=True), IDX_DH)
    o = _rope(col * inv * gki_ref[...], c32, s32, IDX_DH // 2)[:, :IDX_DH]
    ki_o[...] = o
    kib_o[...] = o.astype(BF16)


def _prep(z, tabs, gains, wuk_pairs, *, tm):
    M = z.shape[0]
    row = lambda w: pl.BlockSpec((tm, w), lambda i: (i, 0))
    const = lambda shape: pl.BlockSpec(shape, lambda i: (0,) * len(shape))
    heads = lambda h, w: pl.BlockSpec((h, tm, w), lambda i: (0, i, 0))
    in_specs = [row(NZ)] + [row(LANES)] * 4 + [const((1, LANES))] * 9 + [const(wuk_pairs.shape)]
    out_defs = [
        (A_DH, F32, None), (A_DH, F32, None), (IDX_DH, F32, None), (2 * B_DH, F32, None),
        (2 * B_DH, F32, None), (C_LAT, F32, None), (C_ROPE, F32, None),
        (A_DH, BF16, A_HEADS), (2 * B_DH, BF16, 2 * B_HEADS), (2 * C_LAT, BF16, C_HEADS),
        (IDX_DH, BF16, IDX_HEADS), (LANES, F32, None),
        (A_DH, BF16, None), (A_DH, BF16, None), (IDX_DH, BF16, None), (2 * B_DH, BF16, None),
        (2 * B_DH, BF16, None), (2 * C_LAT, BF16, None),
    ]
    out_specs, out_shape = [], []
    for w, dt, h in out_defs:
        if h is None:
            out_specs.append(row(w))
            out_shape.append(jax.ShapeDtypeStruct((M, w), dt))
        else:
            out_specs.append(heads(h, w))
            out_shape.append(jax.ShapeDtypeStruct((h, M, w), dt))
    return pl.pallas_call(
        _prep_body,
        grid=(M // tm,),
        in_specs=in_specs,
        out_specs=out_specs,
        out_shape=out_shape,
        compiler_params=_cparams(("arbitrary",)),
        name="prep_tokens",
    )(z, *tabs, *gains, wuk_pairs)


def _flash_step(carry, s, v_chunk):
    m, l, acc = carry
    m_new = jnp.maximum(m, jnp.max(s, axis=-1, keepdims=True))
    alpha = jnp.exp2(m - m_new)
    p = jnp.exp2(s - m_new)
    l = alpha * l + jnp.sum(p, axis=-1, keepdims=True)
    acc = alpha * acc + _dot(p.astype(BF16), v_chunk)
    return m_new, l, acc


def _flash_init(rows, dv):
    return (jnp.full((rows, 1), NEG, F32), jnp.zeros((rows, 1), F32), jnp.zeros((rows, dv), F32))


def _causal_bias(i, j, tq, tk):
    qpos = i * tq + lax.broadcasted_iota(jnp.int32, (tq, 1), 0)
    kpos = j * tk + lax.broadcasted_iota(jnp.int32, (1, tk), 1)
    return jnp.where(kpos <= qpos, 0.0, NEG)


def _flash_causal(q, k_ref, v_of, i, *, heads, tq, tk, dv):
    rows = heads * tq

    def chunk(j, carry, bias):
        kc = k_ref[pl.ds(pl.multiple_of(j * tk, tk), tk), :]
        s = _dot_t(q, kc)
        if bias is not None:
            s = (s.reshape(heads, tq, tk) + bias[None]).reshape(rows, tk)
        return _flash_step(carry, s, v_of(j, kc))

    jd = (i * tq) // tk
    carry = lax.fori_loop(0, jd, lambda j, c: chunk(j, c, None), _flash_init(rows, dv))
    _, l, acc = chunk(jd, carry, _causal_bias(i, jd, tq, tk))
    return acc * (1.0 / l)


def _lam_value(lq1, lk1, lq2, lk2, lam_init):
    a = jnp.sum(lq1[...] * lk1[...], axis=-1, keepdims=True)
    b = jnp.sum(lq2[...] * lk2[...], axis=-1, keepdims=True)
    return jnp.exp(a) - jnp.exp(b) + lam_init


def _diff_combine(o, lam, gsub, lam_init):
    outs = []
    for h in range(B_HEADS):
        d = o[h] - lam * o[B_HEADS + h]
        ms = jnp.mean(d * d, axis=-1, keepdims=True)
        outs.append(d * lax.rsqrt(ms + EPS) * gsub * (1.0 - lam_init))
    return outs


def _attn_b_body(mix_ref, lq1, lk1, lq2, lk2, gsub_ref, q_ref, k_ref, v_ref, o_ref, *, tq, tk, lam_init):
    i = pl.program_id(1)
    heads = 2 * B_HEADS
    q = q_ref[...].reshape(heads * tq, 2 * B_DH)

    def v_of(j, kc):
        return v_ref[pl.ds(pl.multiple_of(j * tk, tk), tk), :]

    o = _flash_causal(q, k_ref, v_of, i, heads=heads, tq=tq, tk=tk, dv=2 * B_DH)
    o = o.reshape(heads, tq, 2 * B_DH)
    lam = _lam_value(lq1, lk1, lq2, lk2, lam_init)
    for h, d in enumerate(_diff_combine(o, lam, gsub_ref[...], lam_init)):
        o_ref[:, LANES * h:LANES * (h + 1)] = d.astype(o_ref.dtype)


def _attn_c_body(mix_ref, wuv_ref, q_ref, k_ref, o_ref, *, tq, tk):
    i = pl.program_id(1)
    q = q_ref[...].reshape(C_HEADS * tq, 2 * C_LAT)
    o = _flash_causal(q, k_ref, lambda j, kc: kc[:, :C_LAT], i, heads=C_HEADS, tq=tq, tk=tk, dv=C_LAT)
    o = o.reshape(C_HEADS, tq, C_LAT).astype(BF16)
    for h in range(C_HEADS):
        o_ref[:, C_DV * h:C_DV * (h + 1)] = _dot(o[h], wuv_ref[h]).astype(o_ref.dtype)


def _sortable_key(score):
    b = pltpu.bitcast(score, jnp.int32)
    key = b ^ ((b >> 31) & 0x7FFFFFFF)
    return jnp.where(key == -1, 0, key)


def _threshold_search(count_ge, rows, topk):
    def bit_body(t, thr):
        cand = thr ^ jnp.left_shift(jnp.int32(1), 31 - t)
        return jnp.where(count_ge(cand) >= topk, cand, thr)

    thr = lax.fori_loop(0, 32, bit_body, jnp.full((rows, 1), INT_MIN, jnp.int32))
    return thr, jnp.sum(jnp.where(count_ge(thr) > topk, 1.0, 0.0))


def _attn_a_body(mix_ref, qi_ref, wi_ref, ki_ref, q_ref, k_ref, v_ref, tri_ref, plo_ref, phi_ref, o_ref,
                 key_ref, wb_ref, *, tq, tk, topk):
    i = pl.program_id(1)
    jd = (i * tq) // tk
    nsub = tk // LANES

    w = wi_ref[...]
    for h in range(IDX_HEADS):
        wb_ref[h] = jnp.broadcast_to(w[:, h:h + 1], (tq, LANES))

    def idx_chunk(j, masked):
        kc = ki_ref[pl.ds(pl.multiple_of(j * tk, tk), tk), :]
        score = jnp.zeros((tq, tk), F32)
        for h in range(IDX_HEADS):
            sc = jnp.maximum(_dot_t(qi_ref[h], kc), 0.0)
            score = score + sc * jnp.concatenate([wb_ref[h]] * nsub, axis=-1)
        if masked:
            qpos = i * tq + lax.broadcasted_iota(jnp.int32, (tq, 1), 0)
            kpos = j * tk + lax.broadcasted_iota(jnp.int32, (1, tk), 1)
            score = jnp.where(kpos <= qpos, score, -jnp.inf)
        key_ref[j] = _sortable_key(score)

    def idx_loop(j, c):
        idx_chunk(j, False)
        return c

    lax.fori_loop(0, jd, idx_loop, 0)
    idx_chunk(jd, True)

    rb = min(tq, LANES)

    def count_ge(cand):
        accs = []
        for r0 in range(0, tq, rb):
            candb = jnp.broadcast_to(cand[r0:r0 + rb], (rb, LANES))

            def body(j, acc, r0=r0, candb=candb):
                for c in range(nsub):
                    kk = key_ref[j, r0:r0 + rb, LANES * c:LANES * (c + 1)]
                    acc = acc + jnp.where(kk >= candb, 1.0, 0.0)
                return acc

            accs.append(lax.fori_loop(0, jd + 1, body, jnp.zeros((rb, LANES), F32)))
        acc = accs[0] if len(accs) == 1 else jnp.concatenate(accs, axis=0)
        return jnp.sum(acc, axis=-1, keepdims=True)

    thr, overfull = _threshold_search(count_ge, tq, topk)
    thrb = jnp.broadcast_to(thr, (tq, tk))

    q = q_ref[...].reshape(A_HEADS * tq, A_DH)
    rows = A_HEADS * tq

    def attend(j, fc, bias):
        kc = k_ref[pl.ds(pl.multiple_of(j * tk, tk), tk), :]
        vc = v_ref[pl.ds(pl.multiple_of(j * tk, tk), tk), :]
        s = (_dot_t(q, kc).reshape(A_HEADS, tq, tk) + bias[None]).reshape(rows, tk)
        return _flash_step(fc, s, vc)

    def write(l, acc):
        o = (acc * (1.0 / l)).reshape(A_HEADS, tq, A_DH).astype(BF16)
        for j in range(A_HEADS // 2):
            pair = _dot(o[2 * j], plo_ref[...]) + _dot(o[2 * j + 1], phi_ref[...])
            o_ref[:, LANES * j:LANES * (j + 1)] = pair.astype(o_ref.dtype)

    @pl.when(overfull == 0.0)
    def _():
        def att_chunk(j, fc):
            return attend(j, fc, jnp.where(key_ref[j] >= thrb, 0.0, NEG))

        _, l, acc = lax.fori_loop(0, jd + 1, att_chunk, _flash_init(rows, A_DH))
        write(l, acc)

    @pl.when(overfull > 0.0)
    def _():
        need = topk - count_ge(thr + 1)

        def att_chunk(j, carry):
            ties_seen, fc = carry
            kk = key_ref[j]
            tie = jnp.where(kk == thrb, 1.0, 0.0)
            before = _dot(tie.astype(BF16), tri_ref[...]) + ties_seen
            keep_tie = jnp.where(before < need, tie, 0.0)
            keep = jnp.where(kk > thrb, 1.0, keep_tie)
            keep = jnp.where(kk > KEY_NINF, keep, 0.0)
            bias = jnp.where(keep > 0.0, 0.0, NEG)
            return ties_seen + jnp.sum(tie, axis=-1, keepdims=True), attend(j, fc, bias)

        _, (_, l, acc) = lax.fori_loop(0, jd + 1, att_chunk, (jnp.zeros((tq, 1), F32), _flash_init(rows, A_DH)))
        write(l, acc)


def _place_mats():
    eye = np.eye(A_DH, dtype=np.float32)
    zero = np.zeros((A_DH, A_DH), np.float32)
    return (jnp.asarray(np.concatenate([eye, zero], 1), BF16), jnp.asarray(np.concatenate([zero, eye], 1), BF16))


def _strict_upper(n):
    return jnp.asarray(np.triu(np.ones((n, n), np.float32), 1), BF16)


def _prompt_tiles(S):
    tq = min(256, S)
    tk = min(512, S)
    assert S % tq == 0 and S % tk == 0 and tk % tq == 0
    return tq, tk


def _attn_prompt(mix, t, lw, lam_init, *, nb, S):
    tq, tk = _prompt_tiles(S)
    nq = S // tq
    grid = (nb, nq)
    qspec = lambda h, w: pl.BlockSpec((h, tq, w), lambda n, i: (0, n * nq + i, 0))
    kspec = lambda w: pl.BlockSpec((S, w), lambda n, i: (n, 0))
    const = lambda shape: pl.BlockSpec(shape, lambda n, i: (0,) * len(shape))
    ospec = lambda w, cb: pl.BlockSpec((tq, w), lambda n, i: (n * nq + i, cb))
    whole = pl.BlockSpec(memory_space=pl.ANY)
    out_shape = jax.ShapeDtypeStruct(mix.shape, mix.dtype)
    cp = _cparams(("arbitrary", "arbitrary"))
    topk = min(IDX_TOPK_MAX, S // 4)
    assert tk >= topk
    wa, wb, wc = A_HEADS * A_DH, B_HEADS * 2 * B_DH, C_HEADS * C_DV
    assert wa == wb and wc == wa + wb

    plo, phi = _place_mats()
    mix = pl.pallas_call(
        functools.partial(_attn_a_body, tq=tq, tk=tk, topk=topk),
        grid=grid,
        in_specs=[whole, qspec(IDX_HEADS, IDX_DH), pl.BlockSpec((tq, LANES), lambda n, i: (n * nq + i, 0)),
                  kspec(IDX_DH), qspec(A_HEADS, A_DH), kspec(A_DH), kspec(A_DH),
                  const((tk, tk)), const((A_DH, LANES)), const((A_DH, LANES))],
        out_specs=ospec(wa, 0),
        out_shape=out_shape,
        input_output_aliases={0: 0},
        scratch_shapes=[pltpu.VMEM((S // tk, tq, tk), jnp.int32), pltpu.VMEM((IDX_HEADS, tq, LANES), F32)],
        compiler_params=cp,
        name="attn_a_prompt",
    )(mix, t["qi"], t["wi"], t["kib"], t["qa"], t["kab"], t["vab"], _strict_upper(tk), plo, phi)

    lam_rows = [lw[k].reshape(1, B_DH).astype(F32) for k in ("lam_q1", "lam_k1", "lam_q2", "lam_k2")]
    mix = pl.pallas_call(
        functools.partial(_attn_b_body, tq=tq, tk=tk, lam_init=lam_init),
        grid=grid,
        in_specs=[whole] + [const((1, B_DH))] * 4 + [const((1, 2 * B_DH)), qspec(2 * B_HEADS, 2 * B_DH),
                                                       kspec(2 * B_DH), kspec(2 * B_DH)],
        out_specs=ospec(wb, 1),
        out_shape=out_shape,
        input_output_aliases={0: 0},
        compiler_params=cp,
        name="attn_b_prompt",
    )(mix, *lam_rows, lw["g_subln"].reshape(1, 2 * B_DH), t["qb"], t["kbb"], t["vbb"])

    return pl.pallas_call(
        functools.partial(_attn_c_body, tq=tq, tk=tk),
        grid=grid,
        in_specs=[whole, const((C_HEADS, C_LAT, C_DV)), qspec(C_HEADS, 2 * C_LAT), kspec(2 * C_LAT)],
        out_specs=ospec(wc, 1),
        out_shape=out_shape,
        input_output_aliases={0: 0},
        compiler_params=cp,
        name="attn_c_prompt",
    )(mix, lw["wuv_h"], t["qc"], t["kcb"])


def _page_copies(pt_ref, seq, slot, layer, pools, bufs, sems, keys_on_lanes):
    def copies(p):
        pg = pt_ref[seq, p]
        win = pl.ds(pl.multiple_of(p * PAGE_SIZE, PAGE_SIZE), PAGE_SIZE)
        return [pltpu.make_async_copy(pool.at[layer, pg], buf.at[slot, :, win] if t else buf.at[slot, win],
                                      sem.at[slot])
                for pool, buf, sem, t in zip(pools, bufs, sems, keys_on_lanes)]
    return copies


def _paged_fetch(pt_ref, layer, pools, bufs, sems, keys_on_lanes, npages):
    n = pl.program_id(0)
    nseq = pl.num_programs(0)
    slot = lax.rem(n, 2)

    def start(seq, sl):
        cp = _page_copies(pt_ref, seq, sl, layer, pools, bufs, sems, keys_on_lanes)

        def body(p, c):
            for d in cp(p):
                d.start()
            return c
        lax.fori_loop(0, npages, body, 0)

    @pl.when(n == 0)
    def _():
        start(0, 0)

    @pl.when(n + 1 < nseq)
    def _():
        start(n + 1, 1 - slot)

    cp = _page_copies(pt_ref, n, slot, layer, pools, bufs, sems, keys_on_lanes)

    def wbody(p, c):
        for d in cp(p):
            d.wait()
        return c
    lax.fori_loop(0, npages, wbody, 0)
    return slot


def _row_dot(q, krow):
    return jnp.sum(q.astype(F32) * krow.astype(BF16).astype(F32), axis=-1, keepdims=True)


def _softmax_new(s, s_new, v_past, v_new, v_keys_on_lanes=False):
    m = jnp.maximum(jnp.max(s, axis=-1, keepdims=True), s_new)
    p = jnp.exp2(s - m)
    pn = jnp.exp2(s_new - m)
    l = jnp.sum(p, axis=-1, keepdims=True) + pn
    pv = _dot_t(p.astype(BF16), v_past) if v_keys_on_lanes else _dot(p.astype(BF16), v_past)
    o = pv + pn.astype(BF16).astype(F32) * v_new.astype(BF16).astype(F32)
    return o * (1.0 / l)


def _sattn_b_body(pt_ref, lq1, lk1, lq2, lk2, gsub_ref, q_ref, kn_ref, vn_ref, kpool, vpool, o_ref,
                  kbuf, vbuf, ksem, vsem, *, layer, npages, lam_init):
    slot = _paged_fetch(pt_ref, layer, (kpool, vpool), (kbuf, vbuf), (ksem, vsem), (True, False), npages)
    q = q_ref[0]
    s = _dot(q, kbuf[slot].astype(BF16))
    o = _softmax_new(s, _row_dot(q, kn_ref[0]), vbuf[slot].astype(BF16), vn_ref[0])
    lam = _lam_value(lq1, lk1, lq2, lk2, lam_init)
    d = o[0:B_HEADS] - lam * o[B_HEADS:2 * B_HEADS]
    ms = jnp.mean(d * d, axis=-1, keepdims=True)
    o_ref[0] = (d * lax.rsqrt(ms + EPS) * gsub_ref[...] * (1.0 - lam_init)).astype(o_ref.dtype)


def _sattn_c_body(pt_ref, q_ref, kn_ref, lpool, rpool, o_ref, lbuf, rbuf, lsem, rsem, *, layer, npages):
    slot = _paged_fetch(pt_ref, layer, (lpool, rpool), (lbuf, rbuf), (lsem, rsem), (False, True), npages)
    q = q_ref[0]
    lat = lbuf[slot].astype(BF16)
    s = _dot_t(q[:, :C_LAT], lat) + _dot(q[:, C_LAT:C_LAT + C_ROPE], rbuf[slot].astype(BF16))
    kn = kn_ref[0]
    s_new = jnp.sum(q.astype(F32) * kn.astype(F32), axis=-1, keepdims=True)
    o_ref[0] = _softmax_new(s, s_new, lat, kn[:, :C_LAT]).astype(o_ref.dtype)


def _group_fetch(pt_ref, layer, pool, buf, sem, group, npages):
    n = pl.program_id(0)
    nsteps = pl.num_programs(0)
    slot = lax.rem(n, 2)

    def copy(step, sl, g, p):
        pg = pt_ref[step * group + g, p]
        win = pl.ds(pl.multiple_of(p * PAGE_SIZE, PAGE_SIZE), PAGE_SIZE)
        return pltpu.make_async_copy(pool.at[layer, pg], buf.at[sl, g, :, win], sem.at[sl])

    def start(step, sl):
        def body(p, c):
            for g in range(group):
                copy(step, sl, g, p).start()
            return c
        lax.fori_loop(0, npages, body, 0)

    @pl.when(n == 0)
    def _():
        start(0, 0)

    @pl.when(n + 1 < nsteps)
    def _():
        start(n + 1, 1 - slot)

    def wbody(p, c):
        for g in range(group):
            copy(n, slot, g, p).wait()
        return c
    lax.fori_loop(0, npages, wbody, 0)
    return slot


def _sidx_body(pt_ref, qi_ref, wi_ref, kin_ref, tri_ref, ipool, bias_ref, biasn_ref,
               ibuf, isem, keys_ref, newkey_ref, *, layer, npages, topk, tk, group):
    slot = _group_fetch(pt_ref, layer, ipool, ibuf, isem, group, npages)
    L = npages * PAGE_SIZE
    for g in range(group):
        qi = qi_ref[g]
        w = wi_ref[g]
        sc = jnp.maximum(_dot(qi, ibuf[slot, g].astype(BF16)), 0.0)
        keys_ref[g:g + 1, :] = _sortable_key(jnp.sum(sc * w, axis=0, keepdims=True))
        sc_new = jnp.maximum(_row_dot(qi, kin_ref[g]), 0.0)
        key_new = _sortable_key(jnp.sum(sc_new * w, axis=0, keepdims=True))
        newkey_ref[g:g + 1, :] = jnp.broadcast_to(key_new, (1, LANES))
    keys = keys_ref[...]
    key_new = newkey_ref[:, 0:1]

    def count_ge(cand):
        c = jnp.sum(jnp.where(keys >= cand, 1.0, 0.0), axis=-1, keepdims=True)
        return c + jnp.where(key_new >= cand, 1.0, 0.0)

    thr, overfull = _threshold_search(count_ge, group, topk)
    bias_ref[...] = jnp.where(keys >= thr, 0.0, NEG)
    biasn_ref[...] = jnp.broadcast_to(jnp.where(key_new >= thr, 0.0, NEG), (group, LANES))

    @pl.when(overfull > 0.0)
    def _():
        need = topk - count_ge(thr + 1)
        ties_seen = jnp.zeros((group, 1), F32)
        for c in range(L // tk):
            kk = keys_ref[:, c * tk:(c + 1) * tk]
            tie = jnp.where(kk == thr, 1.0, 0.0)
            before = _dot(tie.astype(BF16), tri_ref[...]) + ties_seen
            keep = jnp.where(kk > thr, 1.0, jnp.where(before < need, tie, 0.0))
            bias_ref[:, c * tk:(c + 1) * tk] = jnp.where(keep > 0.0, 0.0, NEG)
            ties_seen = ties_seen + jnp.sum(tie, axis=-1, keepdims=True)
        keep_new = jnp.where(key_new > thr, 1.0,
                             jnp.where(key_new == thr, jnp.where(ties_seen < need, 1.0, 0.0), 0.0))
        biasn_ref[...] = jnp.broadcast_to(jnp.where(keep_new > 0.0, 0.0, NEG), (group, LANES))


def _sattn_a_body(pt_ref, q_ref, kn_ref, vn_ref, bias_ref, biasn_ref, kpool, vpool, o_ref,
                  kbuf, vbuf, ksem, vsem, *, layer, npages):
    slot = _paged_fetch(pt_ref, layer, (kpool, vpool), (kbuf, vbuf), (ksem, vsem), (True, True), npages)
    q = q_ref[0]
    s = _dot(q, kbuf[slot].astype(BF16)) + bias_ref[0]
    s_new = _row_dot(q, kn_ref[0]) + biasn_ref[0][:, 0:1]
    o_ref[0] = _softmax_new(s, s_new, vbuf[slot].astype(BF16), vn_ref[0], True).astype(o_ref.dtype)


def _attn_sample(ts, lw, caches, page_table, lam_init, layer):
    c_ak, c_av, c_ai, c_bk, c_bv, c_lat, c_kr = caches
    ns, npages = page_table.shape
    L = npages * PAGE_SIZE
    topk = min(IDX_TOPK_MAX, (L + 1) // 4)
    tk = min(512, L)
    assert L % tk == 0
    seq = lambda a, b: pl.BlockSpec((1, a, b), lambda n, pt: (n, 0, 0))
    const = lambda shape: pl.BlockSpec(shape, lambda n, pt: (0,) * len(shape))
    hbm = pl.BlockSpec(memory_space=pl.ANY)
    cp = _cparams(("arbitrary",))
    dsem = pltpu.SemaphoreType.DMA((2,))

    def call(body, in_specs, out_w, out_h, scratch, name, out_dtype=F32):
        return pl.pallas_call(
            body,
            grid_spec=pltpu.PrefetchScalarGridSpec(
                num_scalar_prefetch=1, grid=(ns,), in_specs=in_specs,
                out_specs=seq(out_h, out_w), scratch_shapes=scratch),
            out_shape=jax.ShapeDtypeStruct((ns, out_h, out_w), out_dtype),
            compiler_params=cp,
            name=name,
        )

    group = next((g for g in (2 * SUBLANES, SUBLANES) if ns % g == 0), ns)
    grp = lambda a, b: pl.BlockSpec((group, a, b), lambda n, pt: (n, 0, 0))
    grow = lambda w: pl.BlockSpec((group, w), lambda n, pt: (n, 0))
    bias, bias_new = pl.pallas_call(
        functools.partial(_sidx_body, layer=layer, npages=npages, topk=topk, tk=tk, group=group),
        grid_spec=pltpu.PrefetchScalarGridSpec(
            num_scalar_prefetch=1, grid=(ns // group,),
            in_specs=[grp(IDX_HEADS, IDX_DH), grp(IDX_HEADS, 1), grp(1, IDX_DH), const((tk, tk)), hbm],
            out_specs=[grow(L), grow(LANES)],
            scratch_shapes=[pltpu.VMEM((2, group, IDX_DH, L), F32), dsem,
                            pltpu.VMEM((group, L), jnp.int32), pltpu.VMEM((group, LANES), jnp.int32)]),
        out_shape=[jax.ShapeDtypeStruct((ns, L), F32), jax.ShapeDtypeStruct((ns, LANES), F32)],
        compiler_params=cp,
        name="select_a_sample",
    )(page_table, ts["qi"], ts["wi"], ts["ki"], _strict_upper(tk), c_ai)

    oa = call(
        functools.partial(_sattn_a_body, layer=layer, npages=npages),
        [seq(A_HEADS, A_DH), seq(1, A_DH), seq(1, A_DH), seq(1, L), seq(1, LANES), hbm, hbm],
        A_DH, A_HEADS,
        [pltpu.VMEM((2, A_DH, L), F32), pltpu.VMEM((2, A_DH, L), F32), dsem, dsem],
        "attn_a_sample",
    )(page_table, ts["qa"], ts["ka"], ts["va"], bias.reshape(ns, 1, L), bias_new.reshape(ns, 1, LANES), c_ak, c_av)

    lam_rows = [lw[k].reshape(1, B_DH).astype(F32) for k in ("lam_q1", "lam_k1", "lam_q2", "lam_k2")]
    ob = call(
        functools.partial(_sattn_b_body, layer=layer, npages=npages, lam_init=lam_init),
        [const((1, B_DH))] * 4 + [const((1, 2 * B_DH)), seq(2 * B_HEADS, 2 * B_DH), seq(1, 2 * B_DH),
                                    seq(1, 2 * B_DH), hbm, hbm],
        2 * B_DH, B_HEADS,
        [pltpu.VMEM((2, 2 * B_DH, L), F32), pltpu.VMEM((2, L, 2 * B_DH), F32), dsem, dsem],
        "attn_b_sample",
    )(page_table, *lam_rows, lw["g_subln"].reshape(1, 2 * B_DH), ts["qb"], ts["kb"], ts["vb"], c_bk, c_bv)

    olat = call(
        functools.partial(_sattn_c_body, layer=layer, npages=npages),
        [seq(C_HEADS, 2 * C_LAT), seq(1, 2 * C_LAT), hbm, hbm],
        C_LAT, C_HEADS,
        [pltpu.VMEM((2, L, C_LAT), F32), pltpu.VMEM((2, C_ROPE, L), F32), dsem, dsem],
        "attn_c_sample",
    )(page_table, ts["qc"], ts["kcb"], c_lat, c_kr)

    def uv_body(x_ref, w_ref, o_ref):
        o_ref[...] = _dot(x_ref[0].astype(BF16), w_ref[0]).astype(o_ref.dtype)

    oc = pl.pallas_call(
        uv_body,
        grid=(C_HEADS,),
        in_specs=[pl.BlockSpec((1, ns, C_LAT), lambda h: (h, 0, 0)), pl.BlockSpec((1, C_LAT, C_DV), lambda h: (h, 0, 0))],
        out_specs=pl.BlockSpec((ns, C_DV), lambda h: (0, h)),
        out_shape=jax.ShapeDtypeStruct((ns, C_HEADS * C_DV), BF16),
        compiler_params=_cparams(("arbitrary",)),
        name="uv_proj_sample",
    )(jnp.swapaxes(olat, 0, 1), lw["wuv_h"])
    return jnp.concatenate([oa.reshape(ns, -1).astype(BF16), ob.reshape(ns, -1).astype(BF16), oc], axis=-1)


def _mem_attn_body(q_ref, g_ref, k_ref, v_ref, o_ref):
    n_mem = k_ref.shape[1] // MEM_HEADS
    scale = MEM_DH ** -0.5 * LOG2E
    for b in range(q_ref.shape[0]):
        q = q_ref[b]
        for h in range(MEM_HEADS):
            sl = slice(MEM_DH * h, MEM_DH * (h + 1))
            rows = pl.ds(h, n_mem, stride=MEM_HEADS)
            qh = q[:, sl]
            qh = qh * lax.rsqrt(jnp.mean(qh * qh, axis=-1, keepdims=True) + EPS) * g_ref[...] * scale
            s = _dot_t(qh.astype(BF16), k_ref[b, rows, :].astype(BF16))
            p = jnp.exp2(s - jnp.max(s, axis=-1, keepdims=True))
            l = jnp.sum(p, axis=-1, keepdims=True)
            o = _dot(p.astype(BF16), v_ref[b, rows, :].astype(BF16)) * (1.0 / l)
            o_ref[b, :, sl] = o.astype(o_ref.dtype)


def _mem_attn(q3, g_mq, mk, mv, *, tq, gb=1, kv_layer=None, name):
    G, T, W = q3.shape
    assert G % gb == 0
    if kv_layer is None:
        kvspec = pl.BlockSpec((gb,) + mk.shape[1:], lambda g, i: (g, 0, 0))
    else:
        kvspec = pl.BlockSpec((None, gb) + mk.shape[2:], lambda g, i: (kv_layer, g, 0, 0))
    return pl.pallas_call(
        _mem_attn_body,
        grid=(G // gb, T // tq),
        in_specs=[pl.BlockSpec((gb, tq, W), lambda g, i: (g, i, 0)), pl.BlockSpec((1, MEM_DH), lambda g, i: (0, 0)),
                  kvspec, kvspec],
        out_specs=pl.BlockSpec((gb, tq, W), lambda g, i: (g, i, 0)),
        out_shape=jax.ShapeDtypeStruct((G, T, W), BF16),
        compiler_params=_cparams(("arbitrary", "arbitrary")),
        name=name,
    )(q3, g_mq.reshape(1, MEM_DH), mk, mv)


def _head_norm_body(x_ref, g_ref, o_ref):
    for h in range(MEM_HEADS):
        sl = slice(MEM_DH * h, MEM_DH * (h + 1))
        xh = x_ref[:, sl]
        o_ref[:, sl] = xh * lax.rsqrt(jnp.mean(xh * xh, axis=-1, keepdims=True) + EPS) * g_ref[...]


def _head_norm(x, g):
    M, W = x.shape
    return pl.pallas_call(
        _head_norm_body,
        grid=(1,),
        in_specs=[pl.BlockSpec((M, W), lambda i: (0, 0)), pl.BlockSpec((1, MEM_DH), lambda i: (0, 0))],
        out_specs=pl.BlockSpec((M, W), lambda i: (0, 0)),
        out_shape=jax.ShapeDtypeStruct((M, W), F32),
        compiler_params=_cparams(("arbitrary",)),
        name="mem_key_norm",
    )(x, g.reshape(1, MEM_DH))


def _permute_w_in(w_in):
    L, D, _ = w_in.shape
    o = 0
    parts = {}
    for name, n in (("qa", 512), ("ka", 64), ("va", 64), ("qi", 512), ("ki", 32), ("wi", 16), ("qb", 512),
                    ("kb", 128), ("vb", 128), ("qc", 768), ("ckv", 128), ("kr", 32)):
        parts[name] = w_in[:, :, o:o + n]
        o += n
    qc = parts["qc"].reshape(L, D, C_HEADS, C_NOPE + C_ROPE)
    pad = lambda a: jnp.pad(a, ((0, 0), (0, 0), (0, LANES - a.shape[-1])))
    out = jnp.concatenate([
        parts["qa"], parts["qb"], parts["qi"],
        qc[..., :C_NOPE].reshape(L, D, C_HEADS * C_NOPE), qc[..., C_NOPE:].reshape(L, D, C_HEADS * C_ROPE),
        pad(parts["ka"]), pad(parts["va"]), parts["kb"], parts["vb"], parts["ckv"],
        pad(parts["ki"]), pad(parts["kr"]), pad(parts["wi"])], axis=-1)
    assert out.shape[-1] == NZ
    return out.astype(BF16)


def _rope_tables(pos, head_dim):
    half = head_dim // 2
    inv = ROPE_THETA ** (-jnp.arange(half, dtype=F32) / half)
    ang = pos.astype(F32)[:, None] * inv[None, :]
    cos, sin = jnp.cos(ang), jnp.sin(ang)
    reps = LANES // head_dim
    return jnp.tile(jnp.concatenate([cos, cos], -1), (1, reps)), jnp.tile(jnp.concatenate([-sin, sin], -1), (1, reps))


def _row_tile(m):
    best = None
    for tm in (640, 512, 384, 256, 128):
        padded = -(-m // tm) * tm
        if best is None or padded < best[1]:
            best = (tm, padded)
    return best


def _col_tile(n, cap):
    return max(t for t in range(LANES, min(n, cap) + 1, LANES) if n % t == 0)


def _tile_gain(g, reps, pad_to=None):
    v = jnp.tile(g.astype(F32), reps)
    if pad_to is not None:
        v = jnp.pad(v, (0, pad_to - v.shape[0]))
    return v.reshape(1, -1)


def kernel(x_prompt, x_sample, mem_prompt, cache_a_k, cache_a_v, cache_a_idx, cache_b_k, cache_b_v, cache_c_lat, cache_c_rope, cache_mem_k, cache_mem_v, page_table, norm_mix, w_in, g_qa, g_ka, g_ki, g_qb, g_kb, lam_q1, lam_k1, lam_q2, lam_k2, g_subln, g_qc, g_kv, g_kr, w_uk, w_uv, w_out, norm_cross, norm_memin, w_mq, w_mk, w_mv, g_mq, g_mk, w_mo, norm_mlp, w_up, w_down):
    nb, S, D = x_prompt.shape
    ns, dec_seq, _ = x_sample.shape
    assert dec_seq == 1
    depth = w_in.shape[0]
    npages = page_table.shape[1]
    n_mem = mem_prompt.shape[1]
    n_p = nb * S
    m_tot = n_p + ns
    tm, m_pad = _row_tile(m_tot)

    w_in_p = _permute_w_in(w_in)
    w_out_b, w_mq_b, w_mo_b = w_out.astype(BF16), w_mq.astype(BF16), w_mo.astype(BF16)
    w_mkv_b = jnp.concatenate([w_mk, w_mv], axis=-1).astype(BF16)
    w_up_b, w_down_b = w_up.astype(BF16), w_down.astype(BF16)
    wuv_h = jnp.swapaxes(w_uv, 1, 2).astype(BF16)
    wuk_t = jnp.transpose(w_uk, (0, 2, 3, 1))
    zeros = jnp.zeros_like(wuk_t[:, 0::2])
    wuk_pairs = jnp.concatenate([jnp.concatenate([wuk_t[:, 0::2], zeros], -1),
                                 jnp.concatenate([zeros, wuk_t[:, 1::2]], -1)], axis=-2).astype(BF16)

    pos = jnp.concatenate([jnp.tile(jnp.arange(S, dtype=jnp.int32), nb),
                           jnp.full((m_pad - n_p,), npages * PAGE_SIZE, jnp.int32)])
    tabs = _rope_tables(pos, A_DH) + _rope_tables(pos, IDX_DH)
    slab = min(-(-ns // LANES) * LANES, m_pad - n_p)
    tabs_s = tuple(t[n_p:n_p + slab] for t in tabs)

    tpage = lambda c: jnp.swapaxes(c, 2, 3)
    cbk = jnp.transpose(cache_b_k, (0, 1, 3, 4, 2)).reshape(cache_b_k.shape[:2] + (2 * B_DH, PAGE_SIZE))
    caches = (tpage(cache_a_k), tpage(cache_a_v), tpage(cache_a_idx), cbk, cache_b_v, cache_c_lat,
              tpage(cache_c_rope))
    mem_w = MEM_HEADS * MEM_DH
    cmk = cache_mem_k.reshape(depth, ns, n_mem * MEM_HEADS, MEM_DH)
    cmv = cache_mem_v.reshape(depth, ns, n_mem * MEM_HEADS, MEM_DH)
    mem_rows = mem_prompt.reshape(nb * n_mem, D)

    x = jnp.concatenate([x_prompt.reshape(n_p, D), x_sample.reshape(ns, D),
                         jnp.zeros((m_pad - m_tot, D), F32)], axis=0)
    tq_mem = min(512, S)
    rows_out = [[] for _ in range(7)]
    mem_out = [[], []]
    for l in range(depth):
        lam_init = 0.8 - 0.6 * math.exp(-0.3 * l)
        lw = {"lam_q1": lam_q1[l], "lam_k1": lam_k1[l], "lam_q2": lam_q2[l], "lam_k2": lam_k2[l],
              "g_subln": g_subln[l].astype(F32), "wuv_h": wuv_h[l]}
        z = _matmul(x, w_in_p[l], g=norm_mix[l], tm=tm, tn=NZ // 2, name="in_proj")
        gains = [_tile_gain(g_qa[l], 2), _tile_gain(g_qb[l], 2), _tile_gain(g_qc[l][:C_NOPE], 2),
                 _tile_gain(g_qc[l][C_NOPE:], 4), _tile_gain(g_ka[l], 1, LANES), _tile_gain(g_kb[l], 2),
                 _tile_gain(g_kv[l], 1), _tile_gain(g_ki[l], 1, LANES), _tile_gain(g_kr[l], 1, LANES)]
        (ka, va, ki, kb, vb, lat, kr, qa, qb, qc, qi, wi, kab, vab, kib, kbb, vbb, kcb) = _prep(
            z, tabs, gains, wuk_pairs[l], tm=tm)
        for lst, a in zip(rows_out, (ka, va, ki, kb, vb, lat, kr)):
            lst.append(a)

        tp = {"qa": qa, "qb": qb, "qc": qc, "qi": qi, "wi": wi, "kab": kab, "vab": vab, "kib": kib,
              "kbb": kbb, "vbb": vbb, "kcb": kcb}
        mix = _attn_prompt(jnp.zeros((m_pad, D), BF16), tp, lw, lam_init, nb=nb, S=S)
        (ka_s, va_s, ki_s, kb_s, vb_s, _, _, qa_s, qb_s, qc_s, qi_s, wi_s, _, _, _, _, _, kcb_s) = _prep(
            z[n_p:n_p + slab], tabs_s, gains, wuk_pairs[l], tm=slab)
        seq_major = lambda a: jnp.swapaxes(a[:, :ns], 0, 1)
        seq_row = lambda a: a[:ns].reshape(ns, 1, -1)
        ts = {"qa": seq_major(qa_s), "qb": seq_major(qb_s), "qc": seq_major(qc_s), "qi": seq_major(qi_s),
              "wi": wi_s[:ns, :IDX_HEADS].reshape(ns, IDX_HEADS, 1),
              "ki": seq_row(ki_s), "ka": seq_row(ka_s), "va": seq_row(va_s), "kb": seq_row(kb_s),
              "vb": seq_row(vb_s), "kcb": seq_row(kcb_s)}
        mix_s = _attn_sample(ts, lw, caches, page_table, lam_init, l)
        mix = lax.dynamic_update_slice(mix, mix_s, (n_p, 0))
        x = _matmul(mix, w_out_b[l], res=x, tm=tm, tn=_col_tile(D, 1024), name="out_proj")

        kv = _matmul(mem_rows, w_mkv_b[l], g=norm_memin[l], tm=mem_rows.shape[0] // nb, tn=mem_w, name="mem_kv")
        mk_p = _head_norm(kv[:, :mem_w], g_mk[l])
        mv_p = kv[:, mem_w:]
        mem_out[0].append(mk_p)
        mem_out[1].append(mv_p)
        qm = _matmul(x, w_mq_b[l], g=norm_cross[l], tm=tm, tn=mem_w, name="mem_q")
        om_p = _mem_attn(qm[:n_p].reshape(nb, S, mem_w), g_mq[l], mk_p.reshape(nb, n_mem * MEM_HEADS, MEM_DH),
                         mv_p.reshape(nb, n_mem * MEM_HEADS, MEM_DH), tq=tq_mem, name="mem_attn_prompt")
        om_s = _mem_attn(qm[n_p:m_tot].reshape(ns, 1, mem_w), g_mq[l], cmk, cmv, tq=1,
                         gb=SUBLANES if ns % SUBLANES == 0 else 1, kv_layer=l, name="mem_attn_sample")
        om = jnp.concatenate([om_p.reshape(n_p, mem_w), om_s.reshape(ns, mem_w),
                              jnp.zeros((m_pad - m_tot, mem_w), BF16)], axis=0)
        x = _matmul(om, w_mo_b[l], res=x, tm=tm, tn=_col_tile(D, 1024), name="mem_out")

        u = _matmul(x, w_up_b[l], g=norm_mlp[l], act="relu2", out_dtype=BF16, tm=tm,
                    tn=_col_tile(w_up.shape[-1], 1024), name="mlp_up")
        x = _matmul(u, w_down_b[l], res=x, tm=tm, tn=_col_tile(D, 512), name="mlp_down")

    def stack_rows(lst, rows, lead, tail):
        return jnp.stack([a[rows].reshape(lead + tail) for a in lst], axis=0)

    tails = [(A_DH,), (A_DH,), (IDX_DH,), (2, B_DH), (2 * B_DH,), (C_LAT,), (C_ROPE,)]
    prow, srow = slice(0, n_p), slice(n_p, m_tot)
    outs_p = [stack_rows(lst, prow, (nb, S), t) for lst, t in zip(rows_out, tails)]
    outs_s = [stack_rows(lst, srow, (ns, 1), t) for lst, t in zip(rows_out, tails)]
    mems = [jnp.stack([a.reshape(nb, n_mem, MEM_HEADS, MEM_DH) for a in lst], axis=0) for lst in mem_out]
    return (x[:n_p].reshape(nb, S, D), x[n_p:m_tot].reshape(ns, 1, D), *outs_p, *mems, *outs_s)
```
